```python
import math
import jax, jax.numpy as jnp
from jax import lax
import numpy as np

D_MODEL = 1024
BATCH = 16
SEQ = 4096
DEPTH = 2

GRID_W = 64
CTX_LEN = 256
CONV_WIDTH = 512
CONV_K = 31
N_HEADS = 8
QK_DIM = 64
V_DIM = 2 * QK_DIM
ATTN_WIDTH = N_HEADS * V_DIM
QK_COLS = N_HEADS * 2 * QK_DIM
D_FF = 4 * D_MODEL
ROPE_BASE = 10000.0
N_FREQ_AXIS = QK_DIM // 4
Q_BLOCK = 128
EPS = 1e-6

CONV_OFF = 0
Q_OFF = CONV_OFF + 2 * CONV_WIDTH
K_OFF = Q_OFF + QK_COLS
V_OFF = K_OFF + QK_COLS
GATE_OFF = V_OFF + ATTN_WIDTH
IN_COLS = GATE_OFF + 2 * D_MODEL

kernel_name = "hybrid_conformer_diffattn_dit_block"


def rmsnorm(x, g):
    xf = x.astype(jnp.float32)
    y = xf * lax.rsqrt(jnp.mean(xf * xf, axis=-1, keepdims=True) + EPS)
    return (y * g.astype(jnp.float32)).astype(x.dtype)


def layernorm(x, g, b):
    xf = x.astype(jnp.float32)
    mu = jnp.mean(xf, axis=-1, keepdims=True)
    var = jnp.mean(jnp.square(xf - mu), axis=-1, keepdims=True)
    y = (xf - mu) * lax.rsqrt(var + EPS)
    return (y * g.astype(jnp.float32) + b.astype(jnp.float32)).astype(x.dtype)


def modulate(h, shift, scale):
    return h * (1 + scale) + shift


def axial_rope(n_tokens):
    rows = n_tokens // GRID_W
    row = jnp.repeat(jnp.arange(rows, dtype=jnp.float32), GRID_W)
    col = jnp.tile(jnp.arange(GRID_W, dtype=jnp.float32), rows)
    inv = jnp.power(ROPE_BASE, -jnp.arange(N_FREQ_AXIS, dtype=jnp.float32) / N_FREQ_AXIS)
    ang = jnp.concatenate([row[:, None] * inv, col[:, None] * inv], axis=-1)
    return jnp.cos(ang)[:, None, None, :], jnp.sin(ang)[:, None, None, :]


def apply_rope(t, cos, sin):
    cos = cos.astype(t.dtype)
    sin = sin.astype(t.dtype)
    half = QK_DIM // 2
    t1, t2 = t[..., :half], t[..., half:]
    return jnp.concatenate([t1 * cos - t2 * sin, t2 * cos + t1 * sin], axis=-1)


def diff_attention(q, keys, vals, lam, sub_g, lam_init):
    B, L = q.shape[:2]
    nb = L // Q_BLOCK
    scale = QK_DIM ** -0.5
    qb = q.reshape(B, nb, Q_BLOCK, N_HEADS, 2, QK_DIM).transpose(1, 0, 2, 3, 4, 5)

    def block(qi):
        s = jnp.einsum('bqhcd,bkhcd->bhcqk', qi, keys,
                       preferred_element_type=jnp.float32) * scale
        p = jax.nn.softmax(s, axis=-1)
        w = p[:, :, 0] - lam * p[:, :, 1]
        return jnp.einsum('bhqk,bkhv->bqhv', w.astype(vals.dtype), vals)

    out = lax.map(block, qb)
    out = out.transpose(1, 0, 2, 3, 4).reshape(B, L, N_HEADS, V_DIM)
    out = rmsnorm(out, sub_g) * (1.0 - lam_init)
    return out.reshape(B, L, ATTN_WIDTH)


def conformer_conv(u, w_dw, b_dw, ln_g, ln_b):
    a, g = u[..., :CONV_WIDTH], u[..., CONV_WIDTH:]
    z = a * jax.nn.sigmoid(g)
    pad = CONV_K // 2
    z = lax.conv_general_dilated(z, w_dw[:, None, :].astype(z.dtype), window_strides=(1,),
                                 padding=[(pad, pad)], dimension_numbers=('NWC', 'WIO', 'NWC'),
                                 feature_group_count=CONV_WIDTH) + b_dw
    return jax.nn.silu(layernorm(z, ln_g, ln_b))


def merge_branches(conv_in, attn, gate_logits, w_dw, b_dw, ln_g, ln_b,
                   w_conv_out, w_attn_out, b_gate, w_out):
    y_conv = conformer_conv(conv_in, w_dw, b_dw, ln_g, ln_b) @ w_conv_out
    y_attn = attn @ w_attn_out
    g = jax.nn.sigmoid(gate_logits + b_gate)
    g_conv, g_attn = g[..., :D_MODEL], g[..., D_MODEL:]
    return (g_conv * y_conv + g_attn * y_attn) @ w_out


def sqrelu_mlp(h, w1, w2):
    return jnp.square(jax.nn.relu(h @ w1)) @ w2


def setup_inputs(seed: int = 0) -> dict:
    key = jax.random.key(seed)
    ks = jax.random.split(key, 26)
    f = jnp.float32
    nrm = lambda k, shape, s: (jax.random.normal(k, shape, f) * s).astype(f)
    return {
        "x": nrm(ks[0], (BATCH, SEQ, D_MODEL), 1.0),
        "c": nrm(ks[1], (BATCH, D_MODEL), 1.0),
        "ctx": nrm(ks[2], (BATCH, CTX_LEN, D_MODEL), 1.0),
        "c_ctx": nrm(ks[3], (D_MODEL,), 1.0),
        "w_ada": nrm(ks[4], (DEPTH, D_MODEL, 6 * D_MODEL), 0.5 * D_MODEL ** -0.5),
        "b_ada": nrm(ks[5], (DEPTH, 6 * D_MODEL), 0.02),
        "norm1_g": 1.0 + nrm(ks[6], (DEPTH, D_MODEL), 0.02),
        "w_in": nrm(ks[7], (DEPTH, D_MODEL, IN_COLS), D_MODEL ** -0.5),
        "b_gate": nrm(ks[8], (DEPTH, 2 * D_MODEL), 0.02),
        "q_norm_g": 1.0 + nrm(ks[9], (DEPTH, QK_DIM), 0.02),
        "k_norm_g": 1.0 + nrm(ks[10], (DEPTH, QK_DIM), 0.02),
        "lam_q": nrm(ks[11], (DEPTH, 2, QK_DIM), 0.1),
        "lam_k": nrm(ks[12], (DEPTH, 2, QK_DIM), 0.1),
        "attn_norm_g": 1.0 + nrm(ks[13], (DEPTH, V_DIM), 0.02),
        "w_dw": nrm(ks[14], (DEPTH, CONV_K, CONV_WIDTH), CONV_K ** -0.5),
        "b_dw": nrm(ks[15], (DEPTH, CONV_WIDTH), 0.02),
        "conv_ln_g": 1.0 + nrm(ks[16], (DEPTH, CONV_WIDTH), 0.02),
        "conv_ln_b": nrm(ks[17], (DEPTH, CONV_WIDTH), 0.02),
        "w_conv_out": nrm(ks[18], (DEPTH, CONV_WIDTH, D_MODEL), CONV_WIDTH ** -0.5),
        "w_attn_out": nrm(ks[19], (DEPTH, ATTN_WIDTH, D_MODEL), ATTN_WIDTH ** -0.5),
        "w_out": nrm(ks[20], (DEPTH, D_MODEL, D_MODEL), D_MODEL ** -0.5),
        "norm2_g": 1.0 + nrm(ks[21], (DEPTH, D_MODEL), 0.02),
        "w_mlp1": nrm(ks[22], (DEPTH, D_MODEL, D_FF), D_MODEL ** -0.5),
        "w_mlp2": nrm(ks[23], (DEPTH, D_FF, D_MODEL), D_FF ** -0.5),
    }


def reference(x, c, ctx, c_ctx, w_ada, b_ada, norm1_g, w_in, b_gate, q_norm_g, k_norm_g,
              lam_q, lam_k, attn_norm_g, w_dw, b_dw, conv_ln_g, conv_ln_b, w_conv_out,
              w_attn_out, w_out, norm2_g, w_mlp1, w_mlp2):
    B, S, _ = x.shape
    C = ctx.shape[1]
    cos, sin = axial_rope(S)
    silu_c = jax.nn.silu(c)
    silu_cc = jax.nn.silu(c_ctx)
    h_ctx = ctx

    for i in range(DEPTH):
        last = i == DEPTH - 1
        lam_init = 0.8 - 0.6 * math.exp(-0.3 * i)
        lq = lam_q[i].astype(jnp.float32)
        lk = lam_k[i].astype(jnp.float32)
        lam = jnp.exp(jnp.sum(lq[0] * lk[0])) - jnp.exp(jnp.sum(lq[1] * lk[1])) + lam_init

        mx = (silu_c @ w_ada[i] + b_ada[i])[:, None, :]
        mc = silu_cc @ w_ada[i] + b_ada[i]
        sx1, ax1, gx1, sx2, ax2, gx2 = jnp.split(mx, 6, axis=-1)
        sc1, ac1, gc1, sc2, ac2, gc2 = jnp.split(mc, 6, axis=-1)

        hc = modulate(rmsnorm(h_ctx, norm1_g[i]), sc1, ac1)
        if last:
            kv_c = hc @ w_in[i][:, K_OFF:GATE_OFF]
            k_c, v_c = kv_c[..., :QK_COLS], kv_c[..., QK_COLS:]
        else:
            pc = hc @ w_in[i]
            conv_c = pc[..., CONV_OFF:Q_OFF]
            q_c = pc[..., Q_OFF:K_OFF]
            k_c = pc[..., K_OFF:V_OFF]
            v_c = pc[..., V_OFF:GATE_OFF]
            g_c = pc[..., GATE_OFF:]
        k_c = rmsnorm(k_c.reshape(B, C, N_HEADS, 2, QK_DIM), k_norm_g[i])
        v_c = v_c.reshape(B, C, N_HEADS, V_DIM)

        hx = modulate(rmsnorm(x, norm1_g[i]), sx1, ax1)
        px = hx @ w_in[i]
        conv_x = px[..., CONV_OFF:Q_OFF]
        q_x = px[..., Q_OFF:K_OFF].reshape(B, S, N_HEADS, 2, QK_DIM)
        k_x = px[..., K_OFF:V_OFF].reshape(B, S, N_HEADS, 2, QK_DIM)
        v_x = px[..., V_OFF:GATE_OFF].reshape(B, S, N_HEADS, V_DIM)
        g_x = px[..., GATE_OFF:]
        q_x = apply_rope(rmsnorm(q_x, q_norm_g[i]), cos, sin)
        k_x = apply_rope(rmsnorm(k_x, k_norm_g[i]), cos, sin)
        keys = jnp.concatenate([k_x, k_c], axis=1)
        vals = jnp.concatenate([v_x, v_c], axis=1)
        attn_x = diff_attention(q_x, keys, vals, lam, attn_norm_g[i], lam_init)
        y_x = merge_branches(conv_x, attn_x, g_x, w_dw[i], b_dw[i], conv_ln_g[i], conv_ln_b[i],
                             w_conv_out[i], w_attn_out[i], b_gate[i], w_out[i])
        x = x + gx1 * y_x

        if not last:
            q_c = rmsnorm(q_c.reshape(B, C, N_HEADS, 2, QK_DIM), q_norm_g[i])
            attn_c = diff_attention(q_c, k_c, v_c, lam, attn_norm_g[i], lam_init)
            y_c = merge_branches(conv_c, attn_c, g_c, w_dw[i], b_dw[i], conv_ln_g[i],
                                 conv_ln_b[i], w_conv_out[i], w_attn_out[i], b_gate[i], w_out[i])
            h_ctx = h_ctx + gc1 * y_c

        x = x + gx2 * sqrelu_mlp(modulate(rmsnorm(x, norm2_g[i]), sx2, ax2), w_mlp1[i], w_mlp2[i])
        if not last:
            h_ctx = h_ctx + gc2 * sqrelu_mlp(modulate(rmsnorm(h_ctx, norm2_g[i]), sc2, ac2),
                                             w_mlp1[i], w_mlp2[i])
    return x
```

```python
import functools
import math

import jax
import jax.numpy as jnp
from jax import lax
from jax.experimental import pallas as pl
from jax.experimental.pallas import tpu as pltpu

D_MODEL = 1024
GRID_W = 64
CONV_WIDTH = 512
CONV_K = 31
N_HEADS = 8
QK_DIM = 64
V_DIM = 2 * QK_DIM
ATTN_WIDTH = N_HEADS * V_DIM
QK_COLS = N_HEADS * 2 * QK_DIM
D_FF = 4 * D_MODEL
ROPE_BASE = 10000.0
N_FREQ_AXIS = QK_DIM // 4
EPS = 1e-6

CONV_OFF = 0
Q_OFF = CONV_OFF + 2 * CONV_WIDTH
K_OFF = Q_OFF + QK_COLS
V_OFF = K_OFF + QK_COLS
GATE_OFF = V_OFF + ATTN_WIDTH
IN_COLS = GATE_OFF + 2 * D_MODEL

LANES = 128
SUBLANES = 8
HALO = 16
CONV_PAD = CONV_K // 2
ADA_ROWS = 24
VMEM_LIMIT = 56 * 1024 * 1024

BF16 = jnp.bfloat16
F32 = jnp.float32


def _const_spec(shape):
    zeros = (0,) * len(shape)
    return pl.BlockSpec(shape, lambda *_: zeros, pipeline_mode=pl.Buffered(1))


def _params(n_axes):
    return pltpu.CompilerParams(dimension_semantics=("parallel",) * n_axes,
                                vmem_limit_bytes=VMEM_LIMIT)


def _dot(a, b):
    return jnp.dot(a, b, preferred_element_type=F32)


def _sigmoid(t):
    return 1.0 / (1.0 + jnp.exp(-t))


def _norm_modulate(x, g, shift, scale):
    ms = jnp.mean(x * x, axis=-1, keepdims=True)
    h = x * lax.rsqrt(ms + EPS) * g
    return h * (1.0 + scale) + shift


def _ada_kernel(c_ref, w_ref, b_ref, o_ref):
    cv = c_ref[...]
    s = cv * _sigmoid(cv)
    o_ref[0] = _dot(s, w_ref[0]) + b_ref[0]


def _ada_call(cv, w_ada, b_ada):
    depth = w_ada.shape[0]
    n_col = 6
    return pl.pallas_call(
        _ada_kernel,
        grid=(depth, n_col),
        in_specs=[
            pl.BlockSpec((ADA_ROWS, D_MODEL), lambda i, j: (0, 0)),
            pl.BlockSpec((1, D_MODEL, D_MODEL), lambda i, j: (i, 0, j)),
            pl.BlockSpec((1, 1, D_MODEL), lambda i, j: (i, 0, j)),
        ],
        out_specs=pl.BlockSpec((1, ADA_ROWS, D_MODEL), lambda i, j: (i, 0, j)),
        out_shape=jax.ShapeDtypeStruct((depth, ADA_ROWS, n_col * D_MODEL), F32),
        compiler_params=_params(2),
        name="ada",
    )(cv, w_ada, b_ada.reshape(depth, 1, n_col * D_MODEL))


def _inproj_kernel(x_ref, shift_ref, scale_ref, g_ref, w_ref, bg_ref, qg_ref, kg_ref,
                   gsum_ref, cos_ref, sin_ref,
                   z_ref, qt_ref, k_ref, vt_ref, gate_ref, *, use_rope):
    h = _norm_modulate(x_ref[0], g_ref[...], shift_ref[0], scale_ref[0]).astype(BF16)
    tm = h.shape[0]

    u = _dot(h, w_ref[:, CONV_OFF:Q_OFF])
    z_ref[0] = (u[:, :CONV_WIDTH] * _sigmoid(u[:, CONV_WIDTH:])).astype(BF16)

    lane = lax.broadcasted_iota(jnp.int32, (tm, LANES), 1)
    first_half = (lane & (QK_DIM // 2)) == 0

    def qk_head(t, gain):
        ms = _dot((t * t).astype(BF16), gsum_ref[...])
        t = t * lax.rsqrt(ms + EPS) * gain
        if use_rope:
            partner = jnp.where(first_half,
                                pltpu.roll(t, LANES - QK_DIM // 2, 1),
                                pltpu.roll(t, QK_DIM // 2, 1))
            t = t * cos_ref[...] + partner * sin_ref[...]
        return t

    tq = _dot(h, w_ref[:, Q_OFF:K_OFF])
    for hd in range(N_HEADS):
        t = qk_head(tq[:, hd * LANES:(hd + 1) * LANES], qg_ref[...])
        qt_ref[0, hd] = t.T.astype(BF16)

    tk = _dot(h, w_ref[:, K_OFF:V_OFF])
    for hd in range(N_HEADS):
        t = qk_head(tk[:, hd * LANES:(hd + 1) * LANES], kg_ref[...])
        k_ref[0, :, hd * LANES:(hd + 1) * LANES] = t.astype(BF16)

    tv = _dot(h, w_ref[:, V_OFF:GATE_OFF])
    for hd in range(N_HEADS):
        vt_ref[0, hd, 0] = tv[:, hd * LANES:(hd + 1) * LANES].T.astype(BF16)

    tg = _dot(h, w_ref[:, GATE_OFF:IN_COLS]) + bg_ref[...]
    gate_ref[0] = _sigmoid(tg).astype(BF16)


def _inproj_call(xs, shift, scale, g, w_in, b_gate, qg, kg, gsum, cos, sin, *, tm, use_rope):
    B, L, _ = xs.shape
    nt = L // tm
    row = lambda b, i: (b, i, 0)
    per_batch = pl.BlockSpec((1, 1, D_MODEL), lambda b, i: (b, 0, 0))
    rope_spec = pl.BlockSpec((tm, LANES), lambda b, i: (i, 0))
    return pl.pallas_call(
        functools.partial(_inproj_kernel, use_rope=use_rope),
        grid=(B, nt),
        in_specs=[
            pl.BlockSpec((1, tm, D_MODEL), row),
            per_batch, per_batch,
            _const_spec((1, D_MODEL)),
            _const_spec((D_MODEL, IN_COLS)),
            _const_spec((1, 2 * D_MODEL)),
            _const_spec((1, LANES)), _const_spec((1, LANES)),
            _const_spec((LANES, LANES)),
            rope_spec, rope_spec,
        ],
        out_specs=[
            pl.BlockSpec((1, tm, CONV_WIDTH), row),
            pl.BlockSpec((1, N_HEADS, V_DIM, tm), lambda b, i: (b, 0, 0, i)),
            pl.BlockSpec((1, tm, QK_COLS), row),
            pl.BlockSpec((1, N_HEADS, 1, V_DIM, tm), lambda b, i: (b, 0, i, 0, 0)),
            pl.BlockSpec((1, tm, 2 * D_MODEL), row),
        ],
        out_shape=[
            jax.ShapeDtypeStruct((B, L, CONV_WIDTH), BF16),
            jax.ShapeDtypeStruct((B, N_HEADS, V_DIM, L), BF16),
            jax.ShapeDtypeStruct((B, L, QK_COLS), BF16),
            jax.ShapeDtypeStruct((B, N_HEADS, nt, V_DIM, tm), BF16),
            jax.ShapeDtypeStruct((B, L, 2 * D_MODEL), BF16),
        ],
        compiler_params=_params(2),
        name="inproj",
    )(xs, shift, scale, g, w_in, b_gate, qg, kg, gsum, cos, sin)


def _attn_kernel(*refs, n_src, lam_init):
    qt_ref = refs[0]
    src_refs = refs[1:1 + 2 * n_src]
    lq_ref, lk_ref, ng_ref, o_ref, acc_ref = refs[1 + 2 * n_src:]
    tq = qt_ref.shape[-1]

    qt = qt_ref[0, 0]
    sub = lax.broadcasted_iota(jnp.int32, qt.shape, 0)
    zero = jnp.zeros_like(qt)
    qm = jnp.concatenate([jnp.where(sub < QK_DIM, qt, zero),
                          jnp.where(sub >= QK_DIM, qt, zero)], axis=1)

    acc_ref[...] = jnp.zeros_like(acc_ref)

    def chunk(kc, vtc, m, l):
        s = _dot(kc, qm)
        m_new = jnp.maximum(m, jnp.max(s, axis=0, keepdims=True))
        alpha = jnp.exp(m - m_new)
        p = jnp.exp(s - m_new)
        l = alpha * l + jnp.sum(p, axis=0, keepdims=True)
        acc_ref[...] = alpha * acc_ref[...] + _dot(vtc, p.astype(BF16))
        return m_new, l

    m = jnp.full((1, 2 * tq), -jnp.inf, F32)
    l = jnp.zeros((1, 2 * tq), F32)
    for si in range(n_src):
        k_ref, vt_ref = src_refs[2 * si], src_refs[2 * si + 1]
        n_chunks, tk = vt_ref.shape[2], vt_ref.shape[4]
        if n_chunks == 1:
            m, l = chunk(k_ref[0], vt_ref[0, 0, 0], m, l)
        else:
            def body(ci, carry, k_ref=k_ref, vt_ref=vt_ref, tk=tk):
                start = pl.multiple_of(ci * tk, tk)
                return chunk(k_ref[0, pl.ds(start, tk), :], vt_ref[0, 0, ci], *carry)
            m, l = lax.fori_loop(0, n_chunks, body, (m, l))

    lq = lq_ref[...]
    lk = lk_ref[...]
    dots = jnp.sum(lq * lk, axis=-1, keepdims=True)
    e = jnp.exp(dots)
    lam = e[0:1] - e[1:2] + lam_init

    acc = acc_ref[...]
    inv = 1.0 / l
    o = acc[:, :tq] * inv[:, :tq] - lam * (acc[:, tq:] * inv[:, tq:])
    o = o.T
    ms = jnp.mean(o * o, axis=-1, keepdims=True)
    o = o * lax.rsqrt(ms + EPS) * ng_ref[...] * (1.0 - lam_init)
    o_ref[0] = o.astype(BF16)


def _attn_call(qt, sources, lam_q, lam_k, norm_g, *, tq, lam_init):
    B, _, _, L = qt.shape
    in_specs = [pl.BlockSpec((1, 1, V_DIM, tq), lambda b, h, i: (b, h, 0, i))]
    args = [qt]
    for k, vt in sources:
        T = k.shape[1]
        n_chunks, tk = vt.shape[2], vt.shape[4]
        in_specs.append(pl.BlockSpec((1, T, LANES), lambda b, h, i: (b, 0, h)))
        in_specs.append(pl.BlockSpec((1, 1, n_chunks, V_DIM, tk), lambda b, h, i: (b, h, 0, 0, 0)))
        args += [k, vt]
    in_specs += [_const_spec((2, QK_DIM)), _const_spec((2, QK_DIM)), _const_spec((1, V_DIM))]
    args += [lam_q, lam_k, norm_g]
    return pl.pallas_call(
        functools.partial(_attn_kernel, n_src=len(sources), lam_init=lam_init),
        grid=(B, N_HEADS, L // tq),
        in_specs=in_specs,
        out_specs=pl.BlockSpec((1, tq, V_DIM), lambda b, h, i: (b, i, h)),
        out_shape=jax.ShapeDtypeStruct((B, L, ATTN_WIDTH), BF16),
        scratch_shapes=[pltpu.VMEM((V_DIM, 2 * tq), F32)],
        compiler_params=_params(3),
        name="attn",
    )(*args)


CONV_ROWS = 32


def _merge_kernel(x_ref, z_ref, zprev_ref, znext_ref, attn_ref, gate_ref, g1_ref,
                  wdw_ref, bdw_ref, lng_ref, lnb_ref, wco_ref, wao_ref, wo_ref,
                  o_ref, zext_ref, conv_ref):
    i = pl.program_id(1)
    n = pl.num_programs(1)
    tm = z_ref.shape[1]

    prev = zprev_ref[0].astype(F32)
    nxt = znext_ref[0].astype(F32)
    zext_ref[0, 0:HALO, :] = jnp.where(i > 0, prev, 0.0)
    zext_ref[0, HALO:HALO + tm, :] = z_ref[0].astype(F32)
    zext_ref[0, HALO + tm:, :] = jnp.where(i < n - 1, nxt, 0.0)
    n_shifted = tm + 2 * HALO - SUBLANES
    for b in range(1, SUBLANES):
        zext_ref[b, 0:n_shifted, :] = zext_ref[0, b:b + n_shifted, :]

    def conv_rows(r, carry):
        base = pl.multiple_of(r * CONV_ROWS, CONV_ROWS)
        acc = jnp.broadcast_to(bdw_ref[...], (CONV_ROWS, CONV_WIDTH))
        for k in range(CONV_K):
            off = HALO - CONV_PAD + k
            a, b = divmod(off, SUBLANES)
            acc = acc + (zext_ref[b, pl.ds(base + a * SUBLANES, CONV_ROWS), :]
                         * wdw_ref[k:k + 1, :])
        conv_ref[pl.ds(base, CONV_ROWS), :] = acc
        return carry

    lax.fori_loop(0, tm // CONV_ROWS, conv_rows, 0)

    cv = conv_ref[...]
    mu = jnp.mean(cv, axis=-1, keepdims=True)
    cc = cv - mu
    var = jnp.mean(cc * cc, axis=-1, keepdims=True)
    y = cc * lax.rsqrt(var + EPS) * lng_ref[...] + lnb_ref[...]
    y = y * _sigmoid(y)
    y_conv = _dot(y.astype(BF16), wco_ref[...])
    y_attn = _dot(attn_ref[0], wao_ref[...])
    gates = gate_ref[0].astype(F32)
    merged = gates[:, :D_MODEL] * y_conv + gates[:, D_MODEL:] * y_attn
    y_out = _dot(merged.astype(BF16), wo_ref[...])
    o_ref[0] = x_ref[0] + g1_ref[0] * y_out


def _merge_call(xs, z, attn, gates, g1, w_dw, b_dw, ln_g, ln_b, w_co, w_ao, w_o, *, tm):
    B, L, _ = xs.shape
    nt = L // tm
    hb = tm // HALO
    last_hb = L // HALO - 1
    row = lambda b, i: (b, i, 0)
    return pl.pallas_call(
        _merge_kernel,
        grid=(B, nt),
        in_specs=[
            pl.BlockSpec((1, tm, D_MODEL), row),
            pl.BlockSpec((1, tm, CONV_WIDTH), row),
            pl.BlockSpec((1, HALO, CONV_WIDTH), lambda b, i: (b, jnp.maximum(i * hb - 1, 0), 0)),
            pl.BlockSpec((1, HALO, CONV_WIDTH), lambda b, i: (b, jnp.minimum((i + 1) * hb, last_hb), 0)),
            pl.BlockSpec((1, tm, ATTN_WIDTH), row),
            pl.BlockSpec((1, tm, 2 * D_MODEL), row),
            pl.BlockSpec((1, 1, D_MODEL), lambda b, i: (b, 0, 0)),
            _const_spec((CONV_K, CONV_WIDTH)),
            _const_spec((1, CONV_WIDTH)), _const_spec((1, CONV_WIDTH)), _const_spec((1, CONV_WIDTH)),
            _const_spec((CONV_WIDTH, D_MODEL)),
            _const_spec((ATTN_WIDTH, D_MODEL)),
            _const_spec((D_MODEL, D_MODEL)),
        ],
        out_specs=pl.BlockSpec((1, tm, D_MODEL), row),
        out_shape=jax.ShapeDtypeStruct((B, L, D_MODEL), F32),
        scratch_shapes=[pltpu.VMEM((SUBLANES, tm + 2 * HALO, CONV_WIDTH), F32),
                        pltpu.VMEM((tm, CONV_WIDTH), F32)],
        compiler_params=_params(2),
        name="merge",
    )(xs, z, z, z, attn, gates, g1, w_dw, b_dw, ln_g, ln_b, w_co, w_ao, w_o)


FF_CHUNK = 1024


def _mlp_kernel(x_ref, shift_ref, scale_ref, gate_ref, g_ref, w1_ref, w2_ref, o_ref):
    x = x_ref[0]
    h = _norm_modulate(x, g_ref[...], shift_ref[0], scale_ref[0]).astype(BF16)
    y = jnp.zeros(x.shape, F32)
    for c in range(D_FF // FF_CHUNK):
        u = jnp.maximum(_dot(h, w1_ref[:, c * FF_CHUNK:(c + 1) * FF_CHUNK]), 0.0)
        y = y + _dot((u * u).astype(BF16), w2_ref[c * FF_CHUNK:(c + 1) * FF_CHUNK, :])
    o_ref[0] = x + gate_ref[0] * y


def _mlp_call(xs, shift, scale, gate, g, w1, w2, *, tm):
    B, L, _ = xs.shape
    row = lambda b, i: (b, i, 0)
    per_batch = pl.BlockSpec((1, 1, D_MODEL), lambda b, i: (b, 0, 0))
    return pl.pallas_call(
        _mlp_kernel,
        grid=(B, L // tm),
        in_specs=[
            pl.BlockSpec((1, tm, D_MODEL), row),
            per_batch, per_batch, per_batch,
            _const_spec((1, D_MODEL)),
            _const_spec((D_MODEL, D_FF)),
            _const_spec((D_FF, D_MODEL)),
        ],
        out_specs=pl.BlockSpec((1, tm, D_MODEL), row),
        out_shape=jax.ShapeDtypeStruct((B, L, D_MODEL), F32),
        compiler_params=_params(2),
        name="mlp",
    )(xs, shift, scale, gate, g, w1, w2)


def _rope_tables(n_tokens):
    rows = n_tokens // GRID_W
    row = jnp.repeat(jnp.arange(rows, dtype=F32), GRID_W)
    col = jnp.tile(jnp.arange(GRID_W, dtype=F32), rows)
    inv = jnp.power(ROPE_BASE, -jnp.arange(N_FREQ_AXIS, dtype=F32) / N_FREQ_AXIS)
    ang = jnp.concatenate([row[:, None] * inv, col[:, None] * inv], axis=-1)
    cos, sin = jnp.cos(ang), jnp.sin(ang)
    cos = jnp.tile(cos, (1, 4))
    sin = jnp.tile(jnp.concatenate([-sin, sin], axis=-1), (1, 2))
    return cos, sin


def kernel(x, c, ctx, c_ctx, w_ada, b_ada, norm1_g, w_in, b_gate, q_norm_g, k_norm_g,
           lam_q, lam_k, attn_norm_g, w_dw, b_dw, conv_ln_g, conv_ln_b, w_conv_out,
           w_attn_out, w_out, norm2_g, w_mlp1, w_mlp2):
    B, S, _ = x.shape
    C = ctx.shape[1]
    depth = w_ada.shape[0]
    tm_x, tm_c = 512, C
    tq_x, tq_c = 256, C

    cv = jnp.concatenate([c, c_ctx[None, :], jnp.zeros((ADA_ROWS - B - 1, D_MODEL), F32)], axis=0)
    mod = _ada_call(cv, w_ada, b_ada)

    cos, sin = _rope_tables(S)
    cos_c = jnp.ones((C, LANES), F32)
    sin_c = jnp.zeros((C, LANES), F32)
    grp = jnp.arange(LANES) // QK_DIM
    gsum = jnp.where(grp[:, None] == grp[None, :], 1.0 / QK_DIM, 0.0).astype(BF16)
    qk_scale = QK_DIM ** -0.5

    h_ctx = ctx
    for i in range(depth):
        last = i == depth - 1
        lam_init = 0.8 - 0.6 * math.exp(-0.3 * i)
        mx = [t.reshape(B, 1, D_MODEL) for t in jnp.split(mod[i, :B], 6, axis=-1)]
        mc = [jnp.broadcast_to(t.reshape(1, 1, D_MODEL), (B, 1, D_MODEL))
              for t in jnp.split(mod[i, B], 6, axis=-1)]
        sx1, ax1, gx1, sx2, ax2, gx2 = mx
        sc1, ac1, gc1, sc2, ac2, gc2 = mc

        w_in_i = w_in[i].astype(BF16)
        bg = b_gate[i].reshape(1, 2 * D_MODEL)
        g1 = norm1_g[i].reshape(1, D_MODEL)
        qg = jnp.tile(q_norm_g[i], 2).reshape(1, LANES) * qk_scale
        kg = jnp.tile(k_norm_g[i], 2).reshape(1, LANES)
        inproj = functools.partial(_inproj_call, g=g1, w_in=w_in_i, b_gate=bg, qg=qg, kg=kg,
                                   gsum=gsum)
        merge = functools.partial(
            _merge_call, w_dw=w_dw[i], b_dw=b_dw[i].reshape(1, -1),
            ln_g=conv_ln_g[i].reshape(1, -1), ln_b=conv_ln_b[i].reshape(1, -1),
            w_co=w_conv_out[i].astype(BF16), w_ao=w_attn_out[i].astype(BF16),
            w_o=w_out[i].astype(BF16))
        mlp = functools.partial(_mlp_call, g=norm2_g[i].reshape(1, D_MODEL),
                                w1=w_mlp1[i].astype(BF16), w2=w_mlp2[i].astype(BF16))
        attn = functools.partial(_attn_call, lam_q=lam_q[i], lam_k=lam_k[i],
                                 norm_g=attn_norm_g[i].reshape(1, V_DIM), lam_init=lam_init)

        z_c, qt_c, k_c, vt_c, gate_c = inproj(h_ctx, sc1, ac1, cos=cos_c, sin=sin_c,
                                              tm=tm_c, use_rope=False)
        z_x, qt_x, k_x, vt_x, gate_x = inproj(x, sx1, ax1, cos=cos, sin=sin,
                                              tm=tm_x, use_rope=True)
        attn_x = attn(qt_x, [(k_x, vt_x), (k_c, vt_c)], tq=tq_x)
        x = merge(x, z_x, attn_x, gate_x, gx1, tm=tm_x)
        if not last:
            attn_c = attn(qt_c, [(k_c, vt_c)], tq=tq_c)
            h_ctx = merge(h_ctx, z_c, attn_c, gate_c, gc1, tm=tm_c)

        x = mlp(x, sx2, ax2, gx2, tm=tm_x)
        if not last:
            h_ctx = mlp(h_ctx, sc2, ac2, gc2, tm=tm_c)
    return x
```

```python
import functools
import math

import jax
import jax.numpy as jnp
from jax import lax
from jax.experimental import pallas as pl
from jax.experimental.pallas import tpu as pltpu

D_MODEL = 1024
GRID_W = 64
CONV_WIDTH = 512
CONV_K = 31
N_HEADS = 8
QK_DIM = 64
V_DIM = 2 * QK_DIM
ATTN_WIDTH = N_HEADS * V_DIM
QK_COLS = N_HEADS * 2 * QK_DIM
D_FF = 4 * D_MODEL
ROPE_BASE = 10000.0
N_FREQ_AXIS = QK_DIM // 4
EPS = 1e-6
LOG2_E = math.log2(math.e)
SAFE_LOG2_SCORE = 60.0
SCORE_BOUND_MARGIN = 1.02

CONV_OFF = 0
Q_OFF = CONV_OFF + 2 * CONV_WIDTH
K_OFF = Q_OFF + QK_COLS
V_OFF = K_OFF + QK_COLS
GATE_OFF = V_OFF + ATTN_WIDTH
IN_COLS = GATE_OFF + 2 * D_MODEL

LANES = 128
SUBLANES = 8
BF16_ROWS = 16
HALO = 16
CONV_PAD = CONV_K // 2
ADA_ROWS = 24
VMEM_LIMIT = 56 * 1024 * 1024

BF16 = jnp.bfloat16
F32 = jnp.float32


def _const_spec(shape):
    zeros = (0,) * len(shape)
    return pl.BlockSpec(shape, lambda *_: zeros, pipeline_mode=pl.Buffered(1))


def _params(n_axes):
    return pltpu.CompilerParams(dimension_semantics=("parallel",) * n_axes,
                                vmem_limit_bytes=VMEM_LIMIT)


def _dot(a, b):
    return jnp.dot(a, b, preferred_element_type=F32)


def _sigmoid(t):
    return 1.0 / (1.0 + jnp.exp(-t))


def _norm_modulate(x, g, shift, scale):
    ms = jnp.mean(x * x, axis=-1, keepdims=True)
    h = x * lax.rsqrt(ms + EPS) * g
    return h * (1.0 + scale) + shift


def _ada_kernel(c_ref, w_ref, b_ref, o_ref):
    cv = c_ref[...]
    s = cv * _sigmoid(cv)
    o_ref[0] = _dot(s, w_ref[0]) + b_ref[0]


def _ada_call(cv, w_ada, b_ada):
    depth = w_ada.shape[0]
    n_col = 6
    return pl.pallas_call(
        _ada_kernel,
        grid=(depth, n_col),
        in_specs=[
            pl.BlockSpec((ADA_ROWS, D_MODEL), lambda i, j: (0, 0)),
            pl.BlockSpec((1, D_MODEL, D_MODEL), lambda i, j: (i, 0, j)),
            pl.BlockSpec((1, 1, D_MODEL), lambda i, j: (i, 0, j)),
        ],
        out_specs=pl.BlockSpec((1, ADA_ROWS, D_MODEL), lambda i, j: (i, 0, j)),
        out_shape=jax.ShapeDtypeStruct((depth, ADA_ROWS, n_col * D_MODEL), F32),
        compiler_params=_params(2),
        name="ada",
    )(cv, w_ada, b_ada.reshape(depth, 1, n_col * D_MODEL))


def _inproj_kernel(x_ref, shift_ref, scale_ref, g_ref, w_ref, bg_ref, qg_ref, kg_ref,
                   gsum_ref, cos_ref, sin_ref,
                   z_ref, qt_ref, k_ref, vt_ref, gate_ref, *, use_rope):
    h = _norm_modulate(x_ref[0], g_ref[...], shift_ref[0], scale_ref[0]).astype(BF16)
    tm = h.shape[0]

    u = _dot(h, w_ref[:, CONV_OFF:Q_OFF])
    z_ref[0] = (u[:, :CONV_WIDTH] * _sigmoid(u[:, CONV_WIDTH:])).astype(BF16)

    lane = lax.broadcasted_iota(jnp.int32, (tm, LANES), 1)
    first_half = (lane & (QK_DIM // 2)) == 0

    def qk_head(t, gain):
        ms = _dot((t * t).astype(BF16), gsum_ref[...])
        t = t * lax.rsqrt(ms + EPS) * gain
        if use_rope:
            partner = jnp.where(first_half,
                                pltpu.roll(t, LANES - QK_DIM // 2, 1),
                                pltpu.roll(t, QK_DIM // 2, 1))
            t = t * cos_ref[...] + partner * sin_ref[...]
        return t

    tq = _dot(h, w_ref[:, Q_OFF:K_OFF])
    for hd in range(N_HEADS):
        t = qk_head(tq[:, hd * LANES:(hd + 1) * LANES], qg_ref[...])
        qt_ref[0, hd] = t.T.astype(BF16)

    tk = _dot(h, w_ref[:, K_OFF:V_OFF])
    for hd in range(N_HEADS):
        t = qk_head(tk[:, hd * LANES:(hd + 1) * LANES], kg_ref[...])
        k_ref[0, :, hd * LANES:(hd + 1) * LANES] = t.astype(BF16)

    tv = _dot(h, w_ref[:, V_OFF:GATE_OFF])
    for hd in range(N_HEADS):
        vt_ref[0, hd, 0] = tv[:, hd * LANES:(hd + 1) * LANES].T.astype(BF16)

    tg = _dot(h, w_ref[:, GATE_OFF:IN_COLS]) + bg_ref[...]
    gate_ref[0] = _sigmoid(tg).astype(BF16)


def _inproj_call(xs, shift, scale, g, w_in, b_gate, qg, kg, gsum, cos, sin, *, tm, use_rope):
    B, L, _ = xs.shape
    nt = L // tm
    row = lambda b, i: (b, i, 0)
    per_batch = pl.BlockSpec((1, 1, D_MODEL), lambda b, i: (b, 0, 0))
    rope_spec = pl.BlockSpec((tm, LANES), lambda b, i: (i, 0))
    return pl.pallas_call(
        functools.partial(_inproj_kernel, use_rope=use_rope),
        grid=(B, nt),
        in_specs=[
            pl.BlockSpec((1, tm, D_MODEL), row),
            per_batch, per_batch,
            _const_spec((1, D_MODEL)),
            _const_spec((D_MODEL, IN_COLS)),
            _const_spec((1, 2 * D_MODEL)),
            _const_spec((1, LANES)), _const_spec((1, LANES)),
            _const_spec((LANES, LANES)),
            rope_spec, rope_spec,
        ],
        out_specs=[
            pl.BlockSpec((1, tm, CONV_WIDTH), row),
            pl.BlockSpec((1, N_HEADS, V_DIM, tm), lambda b, i: (b, 0, 0, i)),
            pl.BlockSpec((1, tm, QK_COLS), row),
            pl.BlockSpec((1, N_HEADS, 1, V_DIM, tm), lambda b, i: (b, 0, i, 0, 0)),
            pl.BlockSpec((1, tm, 2 * D_MODEL), row),
        ],
        out_shape=[
            jax.ShapeDtypeStruct((B, L, CONV_WIDTH), BF16),
            jax.ShapeDtypeStruct((B, N_HEADS, V_DIM, L), BF16),
            jax.ShapeDtypeStruct((B, L, QK_COLS), BF16),
            jax.ShapeDtypeStruct((B, N_HEADS, nt, V_DIM, tm), BF16),
            jax.ShapeDtypeStruct((B, L, 2 * D_MODEL), BF16),
        ],
        compiler_params=_params(2),
        name="inproj",
    )(xs, shift, scale, g, w_in, b_gate, qg, kg, gsum, cos, sin)


def _attn_kernel(*refs, n_src, lam_init, bounded):
    qt_ref = refs[0]
    src_refs = refs[1:1 + 2 * n_src]
    lq_ref, lk_ref, ng_ref, o_ref = refs[1 + 2 * n_src:]
    tq = qt_ref.shape[-1]

    qt = qt_ref[0, 0]
    sub = lax.broadcasted_iota(jnp.int32, qt.shape, 0)
    zero = jnp.zeros_like(qt)
    qm = jnp.concatenate([jnp.where(sub < QK_DIM, qt, zero),
                          jnp.where(sub >= QK_DIM, qt, zero)], axis=1)

    chunks = []
    for si in range(n_src):
        k_ref, vt_ref = src_refs[2 * si], src_refs[2 * si + 1]
        n_chunks, tk = vt_ref.shape[2], vt_ref.shape[4]
        chunks += [(k_ref, vt_ref, ci, tk) for ci in range(n_chunks)]

    def scores(chunk):
        k_ref, _, ci, tk = chunk
        return _dot(k_ref[0, ci * tk:(ci + 1) * tk, :], qm)

    m = None
    acc = None
    s_next = scores(chunks[0])
    for idx, (_, vt_ref, ci, tk) in enumerate(chunks):
        s = s_next
        if idx + 1 < len(chunks):
            s_next = scores(chunks[idx + 1])
        ones_rows = (lax.broadcasted_iota(jnp.int32, (BF16_ROWS, tk), 0) == 0).astype(BF16)
        vtc = jnp.concatenate([vt_ref[0, 0, ci], ones_rows], axis=0)
        if bounded:
            pv = _dot(vtc, jnp.exp2(s).astype(BF16))
            acc = pv if acc is None else acc + pv
        else:
            s_max = jnp.max(s, axis=0, keepdims=True)
            m_new = s_max if m is None else jnp.maximum(m, s_max)
            pv = _dot(vtc, jnp.exp2(s - m_new).astype(BF16))
            acc = pv if acc is None else jnp.exp2(m - m_new) * acc + pv
            m = m_new

    lq = lq_ref[...]
    lk = lk_ref[...]
    dots = jnp.sum(lq * lk, axis=-1, keepdims=True)
    e = jnp.exp(dots)
    lam = e[0:1] - e[1:2] + lam_init

    inv = 1.0 / acc[V_DIM:V_DIM + 1, :]
    o = acc[:V_DIM, :tq] * inv[:, :tq] - lam * (acc[:V_DIM, tq:] * inv[:, tq:])
    o = o.T
    ms = jnp.mean(o * o, axis=-1, keepdims=True)
    o = o * lax.rsqrt(ms + EPS) * ng_ref[...] * (1.0 - lam_init)
    o_ref[0] = o.astype(BF16)


def _attn_call(qt, sources, lam_q, lam_k, norm_g, *, tq, lam_init, bounded):
    B, _, _, L = qt.shape
    in_specs = [pl.BlockSpec((1, 1, V_DIM, tq), lambda b, h, i: (b, h, 0, i))]
    args = [qt]
    for k, vt in sources:
        T = k.shape[1]
        n_chunks, tk = vt.shape[2], vt.shape[4]
        in_specs.append(pl.BlockSpec((1, T, LANES), lambda b, h, i: (b, 0, h)))
        in_specs.append(pl.BlockSpec((1, 1, n_chunks, V_DIM, tk), lambda b, h, i: (b, h, 0, 0, 0)))
        args += [k, vt]
    in_specs += [_const_spec((2, QK_DIM)), _const_spec((2, QK_DIM)), _const_spec((1, V_DIM))]
    args += [lam_q, lam_k, norm_g]
    return pl.pallas_call(
        functools.partial(_attn_kernel, n_src=len(sources), lam_init=lam_init, bounded=bounded),
        grid=(B, N_HEADS, L // tq),
        in_specs=in_specs,
        out_specs=pl.BlockSpec((1, tq, V_DIM), lambda b, h, i: (b, i, h)),
        out_shape=jax.ShapeDtypeStruct((B, L, ATTN_WIDTH), BF16),
        compiler_params=_params(3),
        name="attn",
    )(*args)


CONV_ROWS = 32


def _merge_kernel(x_ref, z_ref, zprev_ref, znext_ref, attn_ref, gate_ref, g1_ref,
                  wdw_ref, bdw_ref, lng_ref, lnb_ref, wco_ref, wao_ref, wo_ref,
                  o_ref, zext_ref, conv_ref):
    i = pl.program_id(1)
    n = pl.num_programs(1)
    tm = z_ref.shape[1]

    prev = zprev_ref[0].astype(F32)
    nxt = znext_ref[0].astype(F32)
    zext_ref[0, 0:HALO, :] = jnp.where(i > 0, prev, 0.0)
    zext_ref[0, HALO:HALO + tm, :] = z_ref[0].astype(F32)
    zext_ref[0, HALO + tm:, :] = jnp.where(i < n - 1, nxt, 0.0)
    n_shifted = tm + 2 * HALO - SUBLANES
    for b in range(1, SUBLANES):
        zext_ref[b, 0:n_shifted, :] = zext_ref[0, b:b + n_shifted, :]

    def conv_rows(r, carry):
        base = pl.multiple_of(r * CONV_ROWS, CONV_ROWS)
        acc = jnp.broadcast_to(bdw_ref[...], (CONV_ROWS, CONV_WIDTH))
        for k in range(CONV_K):
            off = HALO - CONV_PAD + k
            a, b = divmod(off, SUBLANES)
            acc = acc + (zext_ref[b, pl.ds(base + a * SUBLANES, CONV_ROWS), :]
                         * wdw_ref[k:k + 1, :])
        conv_ref[pl.ds(base, CONV_ROWS), :] = acc
        return carry

    lax.fori_loop(0, tm // CONV_ROWS, conv_rows, 0)

    cv = conv_ref[...]
    mu = jnp.mean(cv, axis=-1, keepdims=True)
    cc = cv - mu
    var = jnp.mean(cc * cc, axis=-1, keepdims=True)
    y = cc * lax.rsqrt(var + EPS) * lng_ref[...] + lnb_ref[...]
    y = y * _sigmoid(y)
    y_conv = _dot(y.astype(BF16), wco_ref[...])
    y_attn = _dot(attn_ref[0], wao_ref[...])
    gates = gate_ref[0].astype(F32)
    merged = gates[:, :D_MODEL] * y_conv + gates[:, D_MODEL:] * y_attn
    y_out = _dot(merged.astype(BF16), wo_ref[...])
    o_ref[0] = x_ref[0] + g1_ref[0] * y_out


def _merge_call(xs, z, attn, gates, g1, w_dw, b_dw, ln_g, ln_b, w_co, w_ao, w_o, *, tm):
    B, L, _ = xs.shape
    nt = L // tm
    hb = tm // HALO
    last_hb = L // HALO - 1
    row = lambda b, i: (b, i, 0)
    return pl.pallas_call(
        _merge_kernel,
        grid=(B, nt),
        in_specs=[
            pl.BlockSpec((1, tm, D_MODEL), row),
            pl.BlockSpec((1, tm, CONV_WIDTH), row),
            pl.BlockSpec((1, HALO, CONV_WIDTH), lambda b, i: (b, jnp.maximum(i * hb - 1, 0), 0)),
            pl.BlockSpec((1, HALO, CONV_WIDTH), lambda b, i: (b, jnp.minimum((i + 1) * hb, last_hb), 0)),
            pl.BlockSpec((1, tm, ATTN_WIDTH), row),
            pl.BlockSpec((1, tm, 2 * D_MODEL), row),
            pl.BlockSpec((1, 1, D_MODEL), lambda b, i: (b, 0, 0)),
            _const_spec((CONV_K, CONV_WIDTH)),
            _const_spec((1, CONV_WIDTH)), _const_spec((1, CONV_WIDTH)), _const_spec((1, CONV_WIDTH)),
            _const_spec((CONV_WIDTH, D_MODEL)),
            _const_spec((ATTN_WIDTH, D_MODEL)),
            _const_spec((D_MODEL, D_MODEL)),
        ],
        out_specs=pl.BlockSpec((1, tm, D_MODEL), row),
        out_shape=jax.ShapeDtypeStruct((B, L, D_MODEL), F32),
        scratch_shapes=[pltpu.VMEM((SUBLANES, tm + 2 * HALO, CONV_WIDTH), F32),
                        pltpu.VMEM((tm, CONV_WIDTH), F32)],
        compiler_params=_params(2),
        name="merge",
    )(xs, z, z, z, attn, gates, g1, w_dw, b_dw, ln_g, ln_b, w_co, w_ao, w_o)


FF_CHUNK = 1024


def _mlp_kernel(x_ref, shift_ref, scale_ref, gate_ref, g_ref, w1_ref, w2_ref, o_ref):
    x = x_ref[0]
    h = _norm_modulate(x, g_ref[...], shift_ref[0], scale_ref[0]).astype(BF16)
    y = jnp.zeros(x.shape, F32)
    for c in range(D_FF // FF_CHUNK):
        u = jnp.maximum(_dot(h, w1_ref[:, c * FF_CHUNK:(c + 1) * FF_CHUNK]), 0.0)
        y = y + _dot((u * u).astype(BF16), w2_ref[c * FF_CHUNK:(c + 1) * FF_CHUNK, :])
    o_ref[0] = x + gate_ref[0] * y


def _mlp_call(xs, shift, scale, gate, g, w1, w2, *, tm):
    B, L, _ = xs.shape
    row = lambda b, i: (b, i, 0)
    per_batch = pl.BlockSpec((1, 1, D_MODEL), lambda b, i: (b, 0, 0))
    return pl.pallas_call(
        _mlp_kernel,
        grid=(B, L // tm),
        in_specs=[
            pl.BlockSpec((1, tm, D_MODEL), row),
            per_batch, per_batch, per_batch,
            _const_spec((1, D_MODEL)),
            _const_spec((D_MODEL, D_FF)),
            _const_spec((D_FF, D_MODEL)),
        ],
        out_specs=pl.BlockSpec((1, tm, D_MODEL), row),
        out_shape=jax.ShapeDtypeStruct((B, L, D_MODEL), F32),
        compiler_params=_params(2),
        name="mlp",
    )(xs, shift, scale, gate, g, w1, w2)


def _rope_tables(n_tokens):
    rows = n_tokens // GRID_W
    row = jnp.repeat(jnp.arange(rows, dtype=F32), GRID_W)
    col = jnp.tile(jnp.arange(GRID_W, dtype=F32), rows)
    inv = jnp.power(ROPE_BASE, -jnp.arange(N_FREQ_AXIS, dtype=F32) / N_FREQ_AXIS)
    ang = jnp.concatenate([row[:, None] * inv, col[:, None] * inv], axis=-1)
    cos, sin = jnp.cos(ang), jnp.sin(ang)
    cos = jnp.tile(cos, (1, 4))
    sin = jnp.tile(jnp.concatenate([-sin, sin], axis=-1), (1, 2))
    return cos, sin


def kernel(x, c, ctx, c_ctx, w_ada, b_ada, norm1_g, w_in, b_gate, q_norm_g, k_norm_g,
           lam_q, lam_k, attn_norm_g, w_dw, b_dw, conv_ln_g, conv_ln_b, w_conv_out,
           w_attn_out, w_out, norm2_g, w_mlp1, w_mlp2):
    B, S, _ = x.shape
    C = ctx.shape[1]
    depth = w_ada.shape[0]
    tm_x, tm_c = 512, C
    tq_x, tq_c = 512, C

    cv = jnp.concatenate([c, c_ctx[None, :], jnp.zeros((ADA_ROWS - B - 1, D_MODEL), F32)], axis=0)
    mod = _ada_call(cv, w_ada, b_ada)

    cos, sin = _rope_tables(S)
    cos_c = jnp.ones((C, LANES), F32)
    sin_c = jnp.zeros((C, LANES), F32)
    grp = jnp.arange(LANES) // QK_DIM
    gsum = jnp.where(grp[:, None] == grp[None, :], 1.0 / QK_DIM, 0.0).astype(BF16)
    qk_scale = QK_DIM ** -0.5

    h_ctx = ctx
    for i in range(depth):
        last = i == depth - 1
        lam_init = 0.8 - 0.6 * math.exp(-0.3 * i)
        mx = [t.reshape(B, 1, D_MODEL) for t in jnp.split(mod[i, :B], 6, axis=-1)]
        mc = [jnp.broadcast_to(t.reshape(1, 1, D_MODEL), (B, 1, D_MODEL))
              for t in jnp.split(mod[i, B], 6, axis=-1)]
        sx1, ax1, gx1, sx2, ax2, gx2 = mx
        sc1, ac1, gc1, sc2, ac2, gc2 = mc

        w_in_i = w_in[i].astype(BF16)
        bg = b_gate[i].reshape(1, 2 * D_MODEL)
        g1 = norm1_g[i].reshape(1, D_MODEL)
        qg = jnp.tile(q_norm_g[i], 2).reshape(1, LANES) * (qk_scale * LOG2_E)
        kg = jnp.tile(k_norm_g[i], 2).reshape(1, LANES)
        inproj = functools.partial(_inproj_call, g=g1, w_in=w_in_i, b_gate=bg, qg=qg, kg=kg,
                                   gsum=gsum)
        merge = functools.partial(
            _merge_call, w_dw=w_dw[i], b_dw=b_dw[i].reshape(1, -1),
            ln_g=conv_ln_g[i].reshape(1, -1), ln_b=conv_ln_b[i].reshape(1, -1),
            w_co=w_conv_out[i].astype(BF16), w_ao=w_attn_out[i].astype(BF16),
            w_o=w_out[i].astype(BF16))
        mlp = functools.partial(_mlp_call, g=norm2_g[i].reshape(1, D_MODEL),
                                w1=w_mlp1[i].astype(BF16), w2=w_mlp2[i].astype(BF16))
        attn_any = functools.partial(_attn_call, lam_q=lam_q[i], lam_k=lam_k[i],
                                     norm_g=attn_norm_g[i].reshape(1, V_DIM), lam_init=lam_init)
        score_bound = QK_DIM * jnp.max(jnp.abs(qg)) * jnp.max(jnp.abs(kg)) * SCORE_BOUND_MARGIN

        def attn(qt, sources, *, tq, attn_any=attn_any, score_bound=score_bound):
            return lax.cond(score_bound <= SAFE_LOG2_SCORE,
                            lambda: attn_any(qt, sources, tq=tq, bounded=True),
                            lambda: attn_any(qt, sources, tq=tq, bounded=False))

        z_c, qt_c, k_c, vt_c, gate_c = inproj(h_ctx, sc1, ac1, cos=cos_c, sin=sin_c,
                                              tm=tm_c, use_rope=False)
        z_x, qt_x, k_x, vt_x, gate_x = inproj(x, sx1, ax1, cos=cos, sin=sin,
                                              tm=tm_x, use_rope=True)
        attn_x = attn(qt_x, [(k_x, vt_x), (k_c, vt_c)], tq=tq_x)
        x = merge(x, z_x, attn_x, gate_x, gx1, tm=tm_x)
        if not last:
            attn_c = attn(qt_c, [(k_c, vt_c)], tq=tq_c)
            h_ctx = merge(h_ctx, z_c, attn_c, gate_c, gc1, tm=tm_c)

        x = mlp(x, sx2, ax2, gx2, tm=tm_x)
        if not last:
            h_ctx = mlp(h_ctx, sc2, ac2, gc2, tm=tm_c)
    return x
```

```python
import functools
import math

import jax
import jax.numpy as jnp
from jax import lax
from jax.experimental import pallas as pl
from jax.experimental.pallas import tpu as pltpu

D_MODEL = 1024
GRID_W = 64
CONV_WIDTH = 512
CONV_K = 31
N_HEADS = 8
QK_DIM = 64
V_DIM = 2 * QK_DIM
ATTN_WIDTH = N_HEADS * V_DIM
QK_COLS = N_HEADS * 2 * QK_DIM
D_FF = 4 * D_MODEL
ROPE_BASE = 10000.0
N_FREQ_AXIS = QK_DIM // 4
EPS = 1e-6
LOG2_E = math.log2(math.e)
SAFE_LOG2_SCORE = 60.0
SCORE_BOUND_MARGIN = 1.02

CONV_OFF = 0
Q_OFF = CONV_OFF + 2 * CONV_WIDTH
K_OFF = Q_OFF + QK_COLS
V_OFF = K_OFF + QK_COLS
GATE_OFF = V_OFF + ATTN_WIDTH
IN_COLS = GATE_OFF + 2 * D_MODEL

LANES = 128
SUBLANES = 8
BF16_ROWS = 16
HALO = 16
CONV_PAD = CONV_K // 2
ADA_ROWS = 24
VMEM_LIMIT = 56 * 1024 * 1024

BF16 = jnp.bfloat16
F32 = jnp.float32


def _const_spec(shape):
    zeros = (0,) * len(shape)
    return pl.BlockSpec(shape, lambda *_: zeros, pipeline_mode=pl.Buffered(1))


def _params(n_axes):
    return pltpu.CompilerParams(dimension_semantics=("parallel",) * n_axes,
                                vmem_limit_bytes=VMEM_LIMIT)


def _dot(a, b):
    return jnp.dot(a, b, preferred_element_type=F32)


def _sigmoid(t):
    return 1.0 / (1.0 + jnp.exp(-t))


def _norm_modulate(x, g, shift, scale):
    ms = jnp.mean(x * x, axis=-1, keepdims=True)
    h = x * lax.rsqrt(ms + EPS) * g
    return h * (1.0 + scale) + shift


def _ada_kernel(c_ref, w_ref, b_ref, o_ref):
    cv = c_ref[...]
    s = cv * _sigmoid(cv)
    o_ref[0] = _dot(s, w_ref[0]) + b_ref[0]


def _ada_call(cv, w_ada, b_ada):
    depth = w_ada.shape[0]
    n_col = 6
    return pl.pallas_call(
        _ada_kernel,
        grid=(depth, n_col),
        in_specs=[
            pl.BlockSpec((ADA_ROWS, D_MODEL), lambda i, j: (0, 0)),
            pl.BlockSpec((1, D_MODEL, D_MODEL), lambda i, j: (i, 0, j)),
            pl.BlockSpec((1, 1, D_MODEL), lambda i, j: (i, 0, j)),
        ],
        out_specs=pl.BlockSpec((1, ADA_ROWS, D_MODEL), lambda i, j: (i, 0, j)),
        out_shape=jax.ShapeDtypeStruct((depth, ADA_ROWS, n_col * D_MODEL), F32),
        compiler_params=_params(2),
        name="ada",
    )(cv, w_ada, b_ada.reshape(depth, 1, n_col * D_MODEL))


def _inproj_kernel(x_ref, shift_ref, scale_ref, g_ref, w_ref, bg_ref, qg_ref, kg_ref,
                   gsum_ref, cos_ref, sin_ref, *out_refs, use_rope, kv_only):
    if kv_only:
        k_ref, vt_ref = out_refs
    else:
        z_ref, qt_ref, k_ref, vt_ref, gate_ref = out_refs
    h = _norm_modulate(x_ref[0], g_ref[...], shift_ref[0], scale_ref[0]).astype(BF16)
    tm = h.shape[0]

    lane = lax.broadcasted_iota(jnp.int32, (tm, LANES), 1)
    first_half = (lane & (QK_DIM // 2)) == 0

    def qk_head(t, gain):
        ms = _dot((t * t).astype(BF16), gsum_ref[...])
        t = t * lax.rsqrt(ms + EPS) * gain
        if use_rope:
            partner = jnp.where(first_half,
                                pltpu.roll(t, LANES - QK_DIM // 2, 1),
                                pltpu.roll(t, QK_DIM // 2, 1))
            t = t * cos_ref[...] + partner * sin_ref[...]
        return t

    if not kv_only:
        u = _dot(h, w_ref[:, CONV_OFF:Q_OFF])
        z_ref[0] = (u[:, :CONV_WIDTH] * _sigmoid(u[:, CONV_WIDTH:])).astype(BF16)

        tq = _dot(h, w_ref[:, Q_OFF:K_OFF])
        for hd in range(N_HEADS):
            t = qk_head(tq[:, hd * LANES:(hd + 1) * LANES], qg_ref[...])
            qt_ref[0, hd] = t.T.astype(BF16)

    tk = _dot(h, w_ref[:, K_OFF:V_OFF])
    for hd in range(N_HEADS):
        t = qk_head(tk[:, hd * LANES:(hd + 1) * LANES], kg_ref[...])
        k_ref[0, :, hd * LANES:(hd + 1) * LANES] = t.astype(BF16)

    tv = _dot(h, w_ref[:, V_OFF:GATE_OFF])
    for hd in range(N_HEADS):
        vt_ref[0, hd, 0] = tv[:, hd * LANES:(hd + 1) * LANES].T.astype(BF16)

    if not kv_only:
        tg = _dot(h, w_ref[:, GATE_OFF:IN_COLS]) + bg_ref[...]
        gate_ref[0] = _sigmoid(tg).astype(BF16)


def _inproj_call(xs, shift, scale, g, w_in, b_gate, qg, kg, gsum, cos, sin, *,
                 tm, use_rope, kv_only=False):
    B, L, _ = xs.shape
    nt = L // tm
    row = lambda b, i: (b, i, 0)
    per_batch = pl.BlockSpec((1, 1, D_MODEL), lambda b, i: (b, 0, 0))
    rope_spec = pl.BlockSpec((tm, LANES), lambda b, i: (i, 0))
    k_spec = pl.BlockSpec((1, tm, QK_COLS), row)
    k_shape = jax.ShapeDtypeStruct((B, L, QK_COLS), BF16)
    vt_spec = pl.BlockSpec((1, N_HEADS, 1, V_DIM, tm), lambda b, i: (b, 0, i, 0, 0))
    vt_shape = jax.ShapeDtypeStruct((B, N_HEADS, nt, V_DIM, tm), BF16)
    if kv_only:
        out_specs, out_shape = [k_spec, vt_spec], [k_shape, vt_shape]
    else:
        out_specs = [
            pl.BlockSpec((1, tm, CONV_WIDTH), row),
            pl.BlockSpec((1, N_HEADS, V_DIM, tm), lambda b, i: (b, 0, 0, i)),
            k_spec, vt_spec,
            pl.BlockSpec((1, tm, 2 * D_MODEL), row),
        ]
        out_shape = [
            jax.ShapeDtypeStruct((B, L, CONV_WIDTH), BF16),
            jax.ShapeDtypeStruct((B, N_HEADS, V_DIM, L), BF16),
            k_shape, vt_shape,
            jax.ShapeDtypeStruct((B, L, 2 * D_MODEL), BF16),
        ]
    return pl.pallas_call(
        functools.partial(_inproj_kernel, use_rope=use_rope, kv_only=kv_only),
        grid=(B, nt),
        in_specs=[
            pl.BlockSpec((1, tm, D_MODEL), row),
            per_batch, per_batch,
            _const_spec((1, D_MODEL)),
            _const_spec((D_MODEL, IN_COLS)),
            _const_spec((1, 2 * D_MODEL)),
            _const_spec((1, LANES)), _const_spec((1, LANES)),
            _const_spec((LANES, LANES)),
            rope_spec, rope_spec,
        ],
        out_specs=out_specs,
        out_shape=out_shape,
        compiler_params=_params(2),
        name="inproj",
    )(xs, shift, scale, g, w_in, b_gate, qg, kg, gsum, cos, sin)


def _attn_kernel(*refs, n_src, lam_init, bounded):
    qt_ref = refs[0]
    src_refs = refs[1:1 + 2 * n_src]
    lq_ref, lk_ref, ng_ref, o_ref = refs[1 + 2 * n_src:]
    tq = qt_ref.shape[-1]

    qt = qt_ref[0, 0]
    sub = lax.broadcasted_iota(jnp.int32, qt.shape, 0)
    zero = jnp.zeros_like(qt)
    qm = jnp.concatenate([jnp.where(sub < QK_DIM, qt, zero),
                          jnp.where(sub >= QK_DIM, qt, zero)], axis=1)

    chunks = []
    for si in range(n_src):
        k_ref, vt_ref = src_refs[2 * si], src_refs[2 * si + 1]
        n_chunks, tk = vt_ref.shape[2], vt_ref.shape[4]
        chunks += [(k_ref, vt_ref, ci, tk) for ci in range(n_chunks)]

    def scores(chunk):
        k_ref, _, ci, tk = chunk
        return _dot(k_ref[0, ci * tk:(ci + 1) * tk, :], qm)

    m = None
    acc = None
    s_next = scores(chunks[0])
    for idx, (_, vt_ref, ci, tk) in enumerate(chunks):
        s = s_next
        if idx + 1 < len(chunks):
            s_next = scores(chunks[idx + 1])
        ones_rows = (lax.broadcasted_iota(jnp.int32, (BF16_ROWS, tk), 0) == 0).astype(BF16)
        vtc = jnp.concatenate([vt_ref[0, 0, ci], ones_rows], axis=0)
        if bounded:
            pv = _dot(vtc, jnp.exp2(s).astype(BF16))
            acc = pv if acc is None else acc + pv
        else:
            s_max = jnp.max(s, axis=0, keepdims=True)
            m_new = s_max if m is None else jnp.maximum(m, s_max)
            pv = _dot(vtc, jnp.exp2(s - m_new).astype(BF16))
            acc = pv if acc is None else jnp.exp2(m - m_new) * acc + pv
            m = m_new

    lq = lq_ref[...]
    lk = lk_ref[...]
    dots = jnp.sum(lq * lk, axis=-1, keepdims=True)
    e = jnp.exp(dots)
    lam = e[0:1] - e[1:2] + lam_init

    inv = 1.0 / acc[V_DIM:V_DIM + 1, :]
    o = acc[:V_DIM, :tq] * inv[:, :tq] - lam * (acc[:V_DIM, tq:] * inv[:, tq:])
    o = o.T
    ms = jnp.mean(o * o, axis=-1, keepdims=True)
    o = o * lax.rsqrt(ms + EPS) * ng_ref[...] * (1.0 - lam_init)
    o_ref[0] = o.astype(BF16)


def _attn_call(qt, sources, lam_q, lam_k, norm_g, *, tq, lam_init, bounded):
    B, _, _, L = qt.shape
    in_specs = [pl.BlockSpec((1, 1, V_DIM, tq), lambda b, h, i: (b, h, 0, i))]
    args = [qt]
    for k, vt in sources:
        T = k.shape[1]
        n_chunks, tk = vt.shape[2], vt.shape[4]
        in_specs.append(pl.BlockSpec((1, T, LANES), lambda b, h, i: (b, 0, h)))
        in_specs.append(pl.BlockSpec((1, 1, n_chunks, V_DIM, tk), lambda b, h, i: (b, h, 0, 0, 0)))
        args += [k, vt]
    in_specs += [_const_spec((2, QK_DIM)), _const_spec((2, QK_DIM)), _const_spec((1, V_DIM))]
    args += [lam_q, lam_k, norm_g]
    return pl.pallas_call(
        functools.partial(_attn_kernel, n_src=len(sources), lam_init=lam_init, bounded=bounded),
        grid=(B, N_HEADS, L // tq),
        in_specs=in_specs,
        out_specs=pl.BlockSpec((1, tq, V_DIM), lambda b, h, i: (b, i, h)),
        out_shape=jax.ShapeDtypeStruct((B, L, ATTN_WIDTH), BF16),
        compiler_params=_params(3),
        name="attn",
    )(*args)


CONV_ROWS = 32
MERGE_ROWS = 256


def _merge_kernel(x_ref, z_ref, zprev_ref, znext_ref, attn_ref, gate_ref, g1_ref,
                  wdw_ref, bdw_ref, lng_ref, lnb_ref, wco_ref, wao_ref, wo_ref,
                  o_ref, zext_ref):
    i = pl.program_id(1)
    n = pl.num_programs(1)
    tm = z_ref.shape[1]

    prev = zprev_ref[0].astype(F32)
    nxt = znext_ref[0].astype(F32)
    zext_ref[0, 0:HALO, :] = jnp.where(i > 0, prev, 0.0)
    zext_ref[0, HALO:HALO + tm, :] = z_ref[0].astype(F32)
    zext_ref[0, HALO + tm:, :] = jnp.where(i < n - 1, nxt, 0.0)
    n_shifted = tm + 2 * HALO - SUBLANES
    for b in range(1, SUBLANES):
        zext_ref[b, 0:n_shifted, :] = zext_ref[0, b:b + n_shifted, :]

    def conv_act(base):
        cv = jnp.broadcast_to(bdw_ref[...], (CONV_ROWS // SUBLANES, SUBLANES, CONV_WIDTH))
        for k in range(CONV_K):
            a, b = divmod(HALO - CONV_PAD + k, SUBLANES)
            start = base + a * SUBLANES
            zk = zext_ref[b, start:start + CONV_ROWS, :]
            cv = cv + zk.reshape(CONV_ROWS // SUBLANES, SUBLANES, CONV_WIDTH) * wdw_ref[k][None]
        cv = cv.reshape(CONV_ROWS, CONV_WIDTH)
        mu = jnp.mean(cv, axis=-1, keepdims=True)
        cc = cv - mu
        var = jnp.mean(cc * cc, axis=-1, keepdims=True)
        y = cc * lax.rsqrt(var + EPS) * lng_ref[...] + lnb_ref[...]
        return (y * _sigmoid(y)).astype(BF16)

    for r0 in range(0, tm, MERGE_ROWS):
        rows = slice(r0, r0 + MERGE_ROWS)
        yact = jnp.concatenate([conv_act(r0 + c) for c in range(0, MERGE_ROWS, CONV_ROWS)], axis=0)
        y_conv = _dot(yact, wco_ref[...])
        y_attn = _dot(attn_ref[0, rows, :], wao_ref[...])
        gates = gate_ref[0, rows, :].astype(F32)
        merged = gates[:, :D_MODEL] * y_conv + gates[:, D_MODEL:] * y_attn
        y_out = _dot(merged.astype(BF16), wo_ref[...])
        o_ref[0, rows, :] = x_ref[0, rows, :] + g1_ref[0] * y_out


def _merge_call(xs, z, attn, gates, g1, w_dw, b_dw, ln_g, ln_b, w_co, w_ao, w_o, *, tm):
    B, L, _ = xs.shape
    nt = L // tm
    hb = tm // HALO
    last_hb = L // HALO - 1
    row = lambda b, i: (b, i, 0)
    return pl.pallas_call(
        _merge_kernel,
        grid=(B, nt),
        in_specs=[
            pl.BlockSpec((1, tm, D_MODEL), row),
            pl.BlockSpec((1, tm, CONV_WIDTH), row),
            pl.BlockSpec((1, HALO, CONV_WIDTH), lambda b, i: (b, jnp.maximum(i * hb - 1, 0), 0)),
            pl.BlockSpec((1, HALO, CONV_WIDTH), lambda b, i: (b, jnp.minimum((i + 1) * hb, last_hb), 0)),
            pl.BlockSpec((1, tm, ATTN_WIDTH), row),
            pl.BlockSpec((1, tm, 2 * D_MODEL), row),
            pl.BlockSpec((1, 1, D_MODEL), lambda b, i: (b, 0, 0)),
            _const_spec((CONV_K, SUBLANES, CONV_WIDTH)),
            _const_spec((1, CONV_WIDTH)), _const_spec((1, CONV_WIDTH)), _const_spec((1, CONV_WIDTH)),
            _const_spec((CONV_WIDTH, D_MODEL)),
            _const_spec((ATTN_WIDTH, D_MODEL)),
            _const_spec((D_MODEL, D_MODEL)),
        ],
        out_specs=pl.BlockSpec((1, tm, D_MODEL), row),
        out_shape=jax.ShapeDtypeStruct((B, L, D_MODEL), F32),
        scratch_shapes=[pltpu.VMEM((SUBLANES, tm + 2 * HALO, CONV_WIDTH), F32)],
        compiler_params=_params(2),
        name="merge",
    )(xs, z, z, z, attn, gates, g1, w_dw, b_dw, ln_g, ln_b, w_co, w_ao, w_o)


FF_CHUNK = 1024


def _mlp_kernel(x_ref, shift_ref, scale_ref, gate_ref, g_ref, w1_ref, w2_ref, o_ref):
    x = x_ref[0]
    h = _norm_modulate(x, g_ref[...], shift_ref[0], scale_ref[0]).astype(BF16)
    y = jnp.zeros(x.shape, F32)
    for c in range(D_FF // FF_CHUNK):
        u = jnp.maximum(_dot(h, w1_ref[:, c * FF_CHUNK:(c + 1) * FF_CHUNK]), 0.0)
        y = y + _dot((u * u).astype(BF16), w2_ref[c * FF_CHUNK:(c + 1) * FF_CHUNK, :])
    o_ref[0] = x + gate_ref[0] * y


def _mlp_call(xs, shift, scale, gate, g, w1, w2, *, tm):
    B, L, _ = xs.shape
    row = lambda b, i: (b, i, 0)
    per_batch = pl.BlockSpec((1, 1, D_MODEL), lambda b, i: (b, 0, 0))
    return pl.pallas_call(
        _mlp_kernel,
        grid=(B, L // tm),
        in_specs=[
            pl.BlockSpec((1, tm, D_MODEL), row),
            per_batch, per_batch, per_batch,
            _const_spec((1, D_MODEL)),
            _const_spec((D_MODEL, D_FF)),
            _const_spec((D_FF, D_MODEL)),
        ],
        out_specs=pl.BlockSpec((1, tm, D_MODEL), row),
        out_shape=jax.ShapeDtypeStruct((B, L, D_MODEL), F32),
        compiler_params=_params(2),
        name="mlp",
    )(xs, shift, scale, gate, g, w1, w2)


def _rope_tables(n_tokens):
    rows = n_tokens // GRID_W
    row = jnp.repeat(jnp.arange(rows, dtype=F32), GRID_W)
    col = jnp.tile(jnp.arange(GRID_W, dtype=F32), rows)
    inv = jnp.power(ROPE_BASE, -jnp.arange(N_FREQ_AXIS, dtype=F32) / N_FREQ_AXIS)
    ang = jnp.concatenate([row[:, None] * inv, col[:, None] * inv], axis=-1)
    cos, sin = jnp.cos(ang), jnp.sin(ang)
    cos = jnp.tile(cos, (1, 4))
    sin = jnp.tile(jnp.concatenate([-sin, sin], axis=-1), (1, 2))
    return cos, sin


def _tiles(seq_len, ctx_len):
    latent = dict(tm=min(512, seq_len), tq=min(1024, seq_len))
    context = dict(tm=ctx_len, tq=ctx_len)
    return latent, context


def kernel(x, c, ctx, c_ctx, w_ada, b_ada, norm1_g, w_in, b_gate, q_norm_g, k_norm_g,
           lam_q, lam_k, attn_norm_g, w_dw, b_dw, conv_ln_g, conv_ln_b, w_conv_out,
           w_attn_out, w_out, norm2_g, w_mlp1, w_mlp2):
    B, S, _ = x.shape
    C = ctx.shape[1]
    depth = w_ada.shape[0]
    tile_x, tile_c = _tiles(S, C)
    tm_x, tq_x = tile_x["tm"], tile_x["tq"]
    tm_c, tq_c = tile_c["tm"], tile_c["tq"]

    cv = jnp.concatenate([c, c_ctx[None, :], jnp.zeros((ADA_ROWS - B - 1, D_MODEL), F32)], axis=0)
    mod = _ada_call(cv, w_ada, b_ada)

    cos, sin = _rope_tables(S)
    cos_c = jnp.ones((C, LANES), F32)
    sin_c = jnp.zeros((C, LANES), F32)
    grp = jnp.arange(LANES) // QK_DIM
    gsum = jnp.where(grp[:, None] == grp[None, :], 1.0 / QK_DIM, 0.0).astype(BF16)
    qk_scale = QK_DIM ** -0.5

    h_ctx = ctx
    for i in range(depth):
        last = i == depth - 1
        lam_init = 0.8 - 0.6 * math.exp(-0.3 * i)
        mx = [t.reshape(B, 1, D_MODEL) for t in jnp.split(mod[i, :B], 6, axis=-1)]
        mc = [jnp.broadcast_to(t.reshape(1, 1, D_MODEL), (B, 1, D_MODEL))
              for t in jnp.split(mod[i, B], 6, axis=-1)]
        sx1, ax1, gx1, sx2, ax2, gx2 = mx
        sc1, ac1, gc1, sc2, ac2, gc2 = mc

        w_in_i = w_in[i].astype(BF16)
        bg = b_gate[i].reshape(1, 2 * D_MODEL)
        g1 = norm1_g[i].reshape(1, D_MODEL)
        qg = jnp.tile(q_norm_g[i], 2).reshape(1, LANES) * (qk_scale * LOG2_E)
        kg = jnp.tile(k_norm_g[i], 2).reshape(1, LANES)
        inproj = functools.partial(_inproj_call, g=g1, w_in=w_in_i, b_gate=bg, qg=qg, kg=kg,
                                   gsum=gsum)
        merge = functools.partial(
            _merge_call,
            w_dw=jnp.broadcast_to(w_dw[i][:, None, :], (CONV_K, SUBLANES, CONV_WIDTH)),
            b_dw=b_dw[i].reshape(1, -1),
            ln_g=conv_ln_g[i].reshape(1, -1), ln_b=conv_ln_b[i].reshape(1, -1),
            w_co=w_conv_out[i].astype(BF16), w_ao=w_attn_out[i].astype(BF16),
            w_o=w_out[i].astype(BF16))
        mlp = functools.partial(_mlp_call, g=norm2_g[i].reshape(1, D_MODEL),
                                w1=w_mlp1[i].astype(BF16), w2=w_mlp2[i].astype(BF16))
        attn_any = functools.partial(_attn_call, lam_q=lam_q[i], lam_k=lam_k[i],
                                     norm_g=attn_norm_g[i].reshape(1, V_DIM), lam_init=lam_init)
        score_bound = QK_DIM * jnp.max(jnp.abs(qg)) * jnp.max(jnp.abs(kg)) * SCORE_BOUND_MARGIN

        def attn(qt, sources, *, tq, attn_any=attn_any, score_bound=score_bound):
            return lax.cond(score_bound <= SAFE_LOG2_SCORE,
                            lambda: attn_any(qt, sources, tq=tq, bounded=True),
                            lambda: attn_any(qt, sources, tq=tq, bounded=False))

        if last:
            k_c, vt_c = inproj(h_ctx, sc1, ac1, cos=cos_c, sin=sin_c, tm=tm_c, use_rope=False,
                               kv_only=True)
        else:
            z_c, qt_c, k_c, vt_c, gate_c = inproj(h_ctx, sc1, ac1, cos=cos_c, sin=sin_c,
                                                  tm=tm_c, use_rope=False)
        z_x, qt_x, k_x, vt_x, gate_x = inproj(x, sx1, ax1, cos=cos, sin=sin,
                                              tm=tm_x, use_rope=True)
        attn_x = attn(qt_x, [(k_x, vt_x), (k_c, vt_c)], tq=tq_x)
        x = merge(x, z_x, attn_x, gate_x, gx1, tm=tm_x)
        if not last:
            attn_c = attn(qt_c, [(k_c, vt_c)], tq=tq_c)
            h_ctx = merge(h_ctx, z_c, attn_c, gate_c, gc1, tm=tm_c)

        x = mlp(x, sx2, ax2, gx2, tm=tm_x)
        if not last:
            h_ctx = mlp(h_ctx, sc2, ac2, gc2, tm=tm_c)
    return x
```

```python
import functools
import math

import jax
import jax.numpy as jnp
from jax import lax
from jax.experimental import pallas as pl
from jax.experimental.pallas import tpu as pltpu

D_MODEL = 1024
GRID_W = 64
CONV_WIDTH = 512
CONV_K = 31
N_HEADS = 8
QK_DIM = 64
V_DIM = 2 * QK_DIM
ATTN_WIDTH = N_HEADS * V_DIM
QK_COLS = N_HEADS * 2 * QK_DIM
D_FF = 4 * D_MODEL
ROPE_BASE = 10000.0
N_FREQ_AXIS = QK_DIM // 4
EPS = 1e-6
LOG2_E = math.log2(math.e)
SAFE_LOG2_SCORE = 60.0
SCORE_BOUND_MARGIN = 1.02

CONV_OFF = 0
Q_OFF = CONV_OFF + 2 * CONV_WIDTH
K_OFF = Q_OFF + QK_COLS
V_OFF = K_OFF + QK_COLS
GATE_OFF = V_OFF + ATTN_WIDTH
IN_COLS = GATE_OFF + 2 * D_MODEL

LANES = 128
SUBLANES = 8
BF16_ROWS = 16
HALO = 16
CONV_PAD = CONV_K // 2
ADA_ROWS = 24
VMEM_LIMIT = 56 * 1024 * 1024

BF16 = jnp.bfloat16
F32 = jnp.float32


def _const_spec(shape):
    zeros = (0,) * len(shape)
    return pl.BlockSpec(shape, lambda *_: zeros, pipeline_mode=pl.Buffered(1))


def _params(n_axes):
    return pltpu.CompilerParams(dimension_semantics=("parallel",) * n_axes,
                                vmem_limit_bytes=VMEM_LIMIT)


def _dot(a, b):
    return jnp.dot(a, b, preferred_element_type=F32)


def _sigmoid(t):
    return 1.0 / (1.0 + jnp.exp(-t))


def _norm_modulate(x, g, shift, scale):
    ms = jnp.mean(x * x, axis=-1, keepdims=True)
    h = x * lax.rsqrt(ms + EPS) * g
    return h * (1.0 + scale) + shift


def _ada_kernel(c_ref, w_ref, b_ref, o_ref):
    cv = c_ref[...]
    s = cv * _sigmoid(cv)
    o_ref[0] = _dot(s, w_ref[0]) + b_ref[0]


def _ada_call(cv, w_ada, b_ada):
    depth = w_ada.shape[0]
    n_col = 6
    return pl.pallas_call(
        _ada_kernel,
        grid=(depth, n_col),
        in_specs=[
            pl.BlockSpec((ADA_ROWS, D_MODEL), lambda i, j: (0, 0)),
            pl.BlockSpec((1, D_MODEL, D_MODEL), lambda i, j: (i, 0, j)),
            pl.BlockSpec((1, 1, D_MODEL), lambda i, j: (i, 0, j)),
        ],
        out_specs=pl.BlockSpec((1, ADA_ROWS, D_MODEL), lambda i, j: (i, 0, j)),
        out_shape=jax.ShapeDtypeStruct((depth, ADA_ROWS, n_col * D_MODEL), F32),
        compiler_params=_params(2),
        name="ada",
    )(cv, w_ada, b_ada.reshape(depth, 1, n_col * D_MODEL))


def _inproj_kernel(x_ref, shift_ref, scale_ref, g_ref, w_ref, bg_ref, qg_ref, kg_ref,
                   gsum_ref, cos_ref, sin_ref, *out_refs, use_rope, kv_only):
    if kv_only:
        k_ref, vt_ref = out_refs
    else:
        z_ref, qt_ref, k_ref, vt_ref, gate_ref = out_refs
    h = _norm_modulate(x_ref[0], g_ref[...], shift_ref[0], scale_ref[0]).astype(BF16)
    tm = h.shape[0]

    lane = lax.broadcasted_iota(jnp.int32, (tm, LANES), 1)
    first_half = (lane & (QK_DIM // 2)) == 0

    def qk_head(t, gain):
        ms = _dot((t * t).astype(BF16), gsum_ref[...])
        t = t * lax.rsqrt(ms + EPS) * gain
        if use_rope:
            partner = jnp.where(first_half,
                                pltpu.roll(t, LANES - QK_DIM // 2, 1),
                                pltpu.roll(t, QK_DIM // 2, 1))
            t = t * cos_ref[...] + partner * sin_ref[...]
        return t

    if not kv_only:
        u = _dot(h, w_ref[:, CONV_OFF:Q_OFF])
        z_ref[0] = (u[:, :CONV_WIDTH] * _sigmoid(u[:, CONV_WIDTH:])).astype(BF16)

        tq = _dot(h, w_ref[:, Q_OFF:K_OFF])
        for hd in range(N_HEADS):
            t = qk_head(tq[:, hd * LANES:(hd + 1) * LANES], qg_ref[...])
            qt_ref[0, hd] = t.T.astype(BF16)

    tk = _dot(h, w_ref[:, K_OFF:V_OFF])
    for hd in range(N_HEADS):
        t = qk_head(tk[:, hd * LANES:(hd + 1) * LANES], kg_ref[...])
        k_ref[0, :, hd * LANES:(hd + 1) * LANES] = t.astype(BF16)

    tv = _dot(h, w_ref[:, V_OFF:GATE_OFF])
    for hd in range(N_HEADS):
        vt_ref[0, hd, 0] = tv[:, hd * LANES:(hd + 1) * LANES].T.astype(BF16)

    if not kv_only:
        tg = _dot(h, w_ref[:, GATE_OFF:IN_COLS]) + bg_ref[...]
        gate_ref[0] = _sigmoid(tg).astype(BF16)


def _inproj_call(xs, shift, scale, g, w_in, b_gate, qg, kg, gsum, cos, sin, *,
                 tm, use_rope, kv_only=False):
    B, L, _ = xs.shape
    nt = L // tm
    row = lambda b, i: (b, i, 0)
    per_batch = pl.BlockSpec((1, 1, D_MODEL), lambda b, i: (b, 0, 0))
    rope_spec = pl.BlockSpec((tm, LANES), lambda b, i: (i, 0))
    k_spec = pl.BlockSpec((1, tm, QK_COLS), row)
    k_shape = jax.ShapeDtypeStruct((B, L, QK_COLS), BF16)
    vt_spec = pl.BlockSpec((1, N_HEADS, 1, V_DIM, tm), lambda b, i: (b, 0, i, 0, 0))
    vt_shape = jax.ShapeDtypeStruct((B, N_HEADS, nt, V_DIM, tm), BF16)
    if kv_only:
        out_specs, out_shape = [k_spec, vt_spec], [k_shape, vt_shape]
    else:
        out_specs = [
            pl.BlockSpec((1, tm, CONV_WIDTH), row),
            pl.BlockSpec((1, N_HEADS, V_DIM, tm), lambda b, i: (b, 0, 0, i)),
            k_spec, vt_spec,
            pl.BlockSpec((1, tm, 2 * D_MODEL), row),
        ]
        out_shape = [
            jax.ShapeDtypeStruct((B, L, CONV_WIDTH), BF16),
            jax.ShapeDtypeStruct((B, N_HEADS, V_DIM, L), BF16),
            k_shape, vt_shape,
            jax.ShapeDtypeStruct((B, L, 2 * D_MODEL), BF16),
        ]
    return pl.pallas_call(
        functools.partial(_inproj_kernel, use_rope=use_rope, kv_only=kv_only),
        grid=(B, nt),
        in_specs=[
            pl.BlockSpec((1, tm, D_MODEL), row),
            per_batch, per_batch,
            _const_spec((1, D_MODEL)),
            _const_spec((D_MODEL, IN_COLS)),
            _const_spec((1, 2 * D_MODEL)),
            _const_spec((1, LANES)), _const_spec((1, LANES)),
            _const_spec((LANES, LANES)),
            rope_spec, rope_spec,
        ],
        out_specs=out_specs,
        out_shape=out_shape,
        compiler_params=_params(2),
        name="inproj",
    )(xs, shift, scale, g, w_in, b_gate, qg, kg, gsum, cos, sin)


def _attn_kernel(*refs, n_src, lam_init, bounded):
    qt_ref = refs[0]
    src_refs = refs[1:1 + 2 * n_src]
    lq_ref, lk_ref, ng_ref, o_ref = refs[1 + 2 * n_src:]
    tq = qt_ref.shape[-1]

    qt = qt_ref[0, 0]
    sub = lax.broadcasted_iota(jnp.int32, qt.shape, 0)
    zero = jnp.zeros_like(qt)
    qm = jnp.concatenate([jnp.where(sub < QK_DIM, qt, zero),
                          jnp.where(sub >= QK_DIM, qt, zero)], axis=1)

    chunks = []
    for si in range(n_src):
        k_ref, vt_ref = src_refs[2 * si], src_refs[2 * si + 1]
        n_chunks, tk = vt_ref.shape[2], vt_ref.shape[4]
        chunks += [(k_ref, vt_ref, ci, tk) for ci in range(n_chunks)]

    def scores(chunk):
        k_ref, _, ci, tk = chunk
        return _dot(k_ref[0, ci * tk:(ci + 1) * tk, :], qm)

    m = None
    acc = None
    s_next = scores(chunks[0])
    for idx, (_, vt_ref, ci, tk) in enumerate(chunks):
        s = s_next
        if idx + 1 < len(chunks):
            s_next = scores(chunks[idx + 1])
        ones_rows = (lax.broadcasted_iota(jnp.int32, (BF16_ROWS, tk), 0) == 0).astype(BF16)
        vtc = jnp.concatenate([vt_ref[0, 0, ci], ones_rows], axis=0)
        if bounded:
            pv = _dot(vtc, jnp.exp2(s).astype(BF16))
            acc = pv if acc is None else acc + pv
        else:
            s_max = jnp.max(s, axis=0, keepdims=True)
            m_new = s_max if m is None else jnp.maximum(m, s_max)
            pv = _dot(vtc, jnp.exp2(s - m_new).astype(BF16))
            acc = pv if acc is None else jnp.exp2(m - m_new) * acc + pv
            m = m_new

    lq = lq_ref[...]
    lk = lk_ref[...]
    dots = jnp.sum(lq * lk, axis=-1, keepdims=True)
    e = jnp.exp(dots)
    lam = e[0:1] - e[1:2] + lam_init

    inv = 1.0 / acc[V_DIM:V_DIM + 1, :]
    o = acc[:V_DIM, :tq] * inv[:, :tq] - lam * (acc[:V_DIM, tq:] * inv[:, tq:])
    ms = jnp.mean(o * o, axis=0, keepdims=True)
    gain = jnp.tile(ng_ref[...], (1, tq // LANES))
    o = o * lax.rsqrt(ms + EPS) * gain * (1.0 - lam_init)
    o_ref[0, 0] = o.astype(BF16)


def _attn_call(qt, sources, lam_q, lam_k, norm_g, *, tq, lam_init, bounded):
    B, _, _, L = qt.shape
    in_specs = [pl.BlockSpec((1, 1, V_DIM, tq), lambda b, h, i: (b, h, 0, i))]
    args = [qt]
    for k, vt in sources:
        T = k.shape[1]
        n_chunks, tk = vt.shape[2], vt.shape[4]
        in_specs.append(pl.BlockSpec((1, T, LANES), lambda b, h, i: (b, 0, h)))
        in_specs.append(pl.BlockSpec((1, 1, n_chunks, V_DIM, tk), lambda b, h, i: (b, h, 0, 0, 0)))
        args += [k, vt]
    in_specs += [_const_spec((2, QK_DIM)), _const_spec((2, QK_DIM)), _const_spec((V_DIM, LANES))]
    args += [lam_q, lam_k, norm_g]
    return pl.pallas_call(
        functools.partial(_attn_kernel, n_src=len(sources), lam_init=lam_init, bounded=bounded),
        grid=(B, N_HEADS, L // tq),
        in_specs=in_specs,
        out_specs=pl.BlockSpec((1, 1, V_DIM, tq), lambda b, h, i: (b, h, 0, i)),
        out_shape=jax.ShapeDtypeStruct((B, N_HEADS, V_DIM, L), BF16),
        compiler_params=_params(3),
        name="attn",
    )(*args)


CONV_ROWS = 32
FF_CHUNK = 1024
MERGE_ROWS = 256


def _post_kernel(x_ref, z_ref, zprev_ref, znext_ref, attn_ref, gate_ref, g1_ref,
                 shift2_ref, scale2_ref, g2_ref,
                 wdw_ref, bdw_ref, lng_ref, lnb_ref, wco_ref, wao_ref, wo_ref,
                 n2_ref, w1_ref, w2_ref, o_ref, zext_ref):
    i = pl.program_id(1)
    n = pl.num_programs(1)
    tm = z_ref.shape[1]

    prev = zprev_ref[0].astype(F32)
    nxt = znext_ref[0].astype(F32)
    zext_ref[0, 0:HALO, :] = jnp.where(i > 0, prev, 0.0)
    zext_ref[0, HALO:HALO + tm, :] = z_ref[0].astype(F32)
    zext_ref[0, HALO + tm:, :] = jnp.where(i < n - 1, nxt, 0.0)
    n_shifted = tm + 2 * HALO - SUBLANES
    for b in range(1, SUBLANES):
        zext_ref[b, 0:n_shifted, :] = zext_ref[0, b:b + n_shifted, :]

    def conv_act(base):
        cv = jnp.broadcast_to(bdw_ref[...], (CONV_ROWS // SUBLANES, SUBLANES, CONV_WIDTH))
        for k in range(CONV_K):
            a, b = divmod(HALO - CONV_PAD + k, SUBLANES)
            start = base + a * SUBLANES
            zk = zext_ref[b, start:start + CONV_ROWS, :]
            cv = cv + zk.reshape(CONV_ROWS // SUBLANES, SUBLANES, CONV_WIDTH) * wdw_ref[k][None]
        cv = cv.reshape(CONV_ROWS, CONV_WIDTH)
        mu = jnp.mean(cv, axis=-1, keepdims=True)
        cc = cv - mu
        var = jnp.mean(cc * cc, axis=-1, keepdims=True)
        y = cc * lax.rsqrt(var + EPS) * lng_ref[...] + lnb_ref[...]
        return (y * _sigmoid(y)).astype(BF16)

    for r0 in range(0, tm, MERGE_ROWS):
        rows = slice(r0, r0 + MERGE_ROWS)
        yact = jnp.concatenate([conv_act(r0 + c) for c in range(0, MERGE_ROWS, CONV_ROWS)], axis=0)
        y_conv = _dot(yact, wco_ref[...])
        attn_rows = jnp.concatenate([attn_ref[0, hd, :, rows].T for hd in range(N_HEADS)], axis=1)
        y_attn = _dot(attn_rows, wao_ref[...])
        gates = gate_ref[0, rows, :].astype(F32)
        merged = gates[:, :D_MODEL] * y_conv + gates[:, D_MODEL:] * y_attn
        y_out = _dot(merged.astype(BF16), wo_ref[...])
        x1 = x_ref[0, rows, :] + g1_ref[0] * y_out
        h = _norm_modulate(x1, n2_ref[...], shift2_ref[0], scale2_ref[0]).astype(BF16)
        y = jnp.zeros(x1.shape, F32)
        for c in range(D_FF // FF_CHUNK):
            u = jnp.maximum(_dot(h, w1_ref[:, c * FF_CHUNK:(c + 1) * FF_CHUNK]), 0.0)
            y = y + _dot((u * u).astype(BF16), w2_ref[c * FF_CHUNK:(c + 1) * FF_CHUNK, :])
        o_ref[0, rows, :] = x1 + g2_ref[0] * y


def _post_call(xs, z, attn, gates, g1, shift2, scale2, g2, w_dw, b_dw, ln_g, ln_b, w_co, w_ao,
               w_o, n2, w1, w2, *, tm):
    B, L, _ = xs.shape
    nt = L // tm
    hb = tm // HALO
    last_hb = L // HALO - 1
    row = lambda b, i: (b, i, 0)
    per_batch = pl.BlockSpec((1, 1, D_MODEL), lambda b, i: (b, 0, 0))
    return pl.pallas_call(
        _post_kernel,
        grid=(B, nt),
        in_specs=[
            pl.BlockSpec((1, tm, D_MODEL), row),
            pl.BlockSpec((1, tm, CONV_WIDTH), row),
            pl.BlockSpec((1, HALO, CONV_WIDTH), lambda b, i: (b, jnp.maximum(i * hb - 1, 0), 0)),
            pl.BlockSpec((1, HALO, CONV_WIDTH), lambda b, i: (b, jnp.minimum((i + 1) * hb, last_hb), 0)),
            pl.BlockSpec((1, N_HEADS, V_DIM, tm), lambda b, i: (b, 0, 0, i)),
            pl.BlockSpec((1, tm, 2 * D_MODEL), row),
            per_batch, per_batch, per_batch, per_batch,
            _const_spec((CONV_K, SUBLANES, CONV_WIDTH)),
            _const_spec((1, CONV_WIDTH)), _const_spec((1, CONV_WIDTH)), _const_spec((1, CONV_WIDTH)),
            _const_spec((CONV_WIDTH, D_MODEL)),
            _const_spec((ATTN_WIDTH, D_MODEL)),
            _const_spec((D_MODEL, D_MODEL)),
            _const_spec((1, D_MODEL)),
            _const_spec((D_MODEL, D_FF)),
            _const_spec((D_FF, D_MODEL)),
        ],
        out_specs=pl.BlockSpec((1, tm, D_MODEL), row),
        out_shape=jax.ShapeDtypeStruct((B, L, D_MODEL), F32),
        scratch_shapes=[pltpu.VMEM((SUBLANES, tm + 2 * HALO, CONV_WIDTH), F32)],
        compiler_params=_params(2),
        name="post",
    )(xs, z, z, z, attn, gates, g1, shift2, scale2, g2, w_dw, b_dw, ln_g, ln_b, w_co, w_ao, w_o,
      n2, w1, w2)


def _mlp_kernel(x_ref, shift_ref, scale_ref, gate_ref, g_ref, w1_ref, w2_ref, o_ref):
    x = x_ref[0]
    h = _norm_modulate(x, g_ref[...], shift_ref[0], scale_ref[0]).astype(BF16)
    y = jnp.zeros(x.shape, F32)
    for c in range(D_FF // FF_CHUNK):
        u = jnp.maximum(_dot(h, w1_ref[:, c * FF_CHUNK:(c + 1) * FF_CHUNK]), 0.0)
        y = y + _dot((u * u).astype(BF16), w2_ref[c * FF_CHUNK:(c + 1) * FF_CHUNK, :])
    o_ref[0] = x + gate_ref[0] * y


def _mlp_call(xs, shift, scale, gate, g, w1, w2, *, tm):
    B, L, _ = xs.shape
    row = lambda b, i: (b, i, 0)
    per_batch = pl.BlockSpec((1, 1, D_MODEL), lambda b, i: (b, 0, 0))
    return pl.pallas_call(
        _mlp_kernel,
        grid=(B, L // tm),
        in_specs=[
            pl.BlockSpec((1, tm, D_MODEL), row),
            per_batch, per_batch, per_batch,
            _const_spec((1, D_MODEL)),
            _const_spec((D_MODEL, D_FF)),
            _const_spec((D_FF, D_MODEL)),
        ],
        out_specs=pl.BlockSpec((1, tm, D_MODEL), row),
        out_shape=jax.ShapeDtypeStruct((B, L, D_MODEL), F32),
        compiler_params=_params(2),
        name="mlp",
    )(xs, shift, scale, gate, g, w1, w2)


def _rope_tables(n_tokens):
    rows = n_tokens // GRID_W
    row = jnp.repeat(jnp.arange(rows, dtype=F32), GRID_W)
    col = jnp.tile(jnp.arange(GRID_W, dtype=F32), rows)
    inv = jnp.power(ROPE_BASE, -jnp.arange(N_FREQ_AXIS, dtype=F32) / N_FREQ_AXIS)
    ang = jnp.concatenate([row[:, None] * inv, col[:, None] * inv], axis=-1)
    cos, sin = jnp.cos(ang), jnp.sin(ang)
    cos = jnp.tile(cos, (1, 4))
    sin = jnp.tile(jnp.concatenate([-sin, sin], axis=-1), (1, 2))
    return cos, sin


def _tiles(seq_len, ctx_len):
    latent = dict(tm=min(512, seq_len), tq=min(1024, seq_len))
    context = dict(tm=ctx_len, tq=ctx_len)
    return latent, context


def kernel(x, c, ctx, c_ctx, w_ada, b_ada, norm1_g, w_in, b_gate, q_norm_g, k_norm_g,
           lam_q, lam_k, attn_norm_g, w_dw, b_dw, conv_ln_g, conv_ln_b, w_conv_out,
           w_attn_out, w_out, norm2_g, w_mlp1, w_mlp2):
    B, S, _ = x.shape
    C = ctx.shape[1]
    depth = w_ada.shape[0]
    tile_x, tile_c = _tiles(S, C)
    tm_x, tq_x = tile_x["tm"], tile_x["tq"]
    tm_c, tq_c = tile_c["tm"], tile_c["tq"]

    cv = jnp.concatenate([c, c_ctx[None, :], jnp.zeros((ADA_ROWS - B - 1, D_MODEL), F32)], axis=0)
    mod = _ada_call(cv, w_ada, b_ada)

    cos, sin = _rope_tables(S)
    cos_c = jnp.ones((C, LANES), F32)
    sin_c = jnp.zeros((C, LANES), F32)
    grp = jnp.arange(LANES) // QK_DIM
    gsum = jnp.where(grp[:, None] == grp[None, :], 1.0 / QK_DIM, 0.0).astype(BF16)
    qk_scale = QK_DIM ** -0.5

    h_ctx = ctx
    for i in range(depth):
        last = i == depth - 1
        lam_init = 0.8 - 0.6 * math.exp(-0.3 * i)
        mx = [t.reshape(B, 1, D_MODEL) for t in jnp.split(mod[i, :B], 6, axis=-1)]
        mc = [jnp.broadcast_to(t.reshape(1, 1, D_MODEL), (B, 1, D_MODEL))
              for t in jnp.split(mod[i, B], 6, axis=-1)]
        sx1, ax1, gx1, sx2, ax2, gx2 = mx
        sc1, ac1, gc1, sc2, ac2, gc2 = mc

        w_in_i = w_in[i].astype(BF16)
        bg = b_gate[i].reshape(1, 2 * D_MODEL)
        g1 = norm1_g[i].reshape(1, D_MODEL)
        qg = jnp.tile(q_norm_g[i], 2).reshape(1, LANES) * (qk_scale * LOG2_E)
        kg = jnp.tile(k_norm_g[i], 2).reshape(1, LANES)
        inproj = functools.partial(_inproj_call, g=g1, w_in=w_in_i, b_gate=bg, qg=qg, kg=kg,
                                   gsum=gsum)
        post = functools.partial(
            _post_call,
            w_dw=jnp.broadcast_to(w_dw[i][:, None, :], (CONV_K, SUBLANES, CONV_WIDTH)),
            b_dw=b_dw[i].reshape(1, -1),
            ln_g=conv_ln_g[i].reshape(1, -1), ln_b=conv_ln_b[i].reshape(1, -1),
            w_co=w_conv_out[i].astype(BF16), w_ao=w_attn_out[i].astype(BF16),
            w_o=w_out[i].astype(BF16), n2=norm2_g[i].reshape(1, D_MODEL),
            w1=w_mlp1[i].astype(BF16), w2=w_mlp2[i].astype(BF16))
        attn_any = functools.partial(_attn_call, lam_q=lam_q[i], lam_k=lam_k[i],
                                     norm_g=jnp.broadcast_to(attn_norm_g[i][:, None], (V_DIM, LANES)),
                                     lam_init=lam_init)
        score_bound = QK_DIM * jnp.max(jnp.abs(qg)) * jnp.max(jnp.abs(kg)) * SCORE_BOUND_MARGIN

        def attn(qt, sources, *, tq, attn_any=attn_any, score_bound=score_bound):
            return lax.cond(score_bound <= SAFE_LOG2_SCORE,
                            lambda: attn_any(qt, sources, tq=tq, bounded=True),
                            lambda: attn_any(qt, sources, tq=tq, bounded=False))

        if last:
            k_c, vt_c = inproj(h_ctx, sc1, ac1, cos=cos_c, sin=sin_c, tm=tm_c, use_rope=False,
                               kv_only=True)
        else:
            z_c, qt_c, k_c, vt_c, gate_c = inproj(h_ctx, sc1, ac1, cos=cos_c, sin=sin_c,
                                                  tm=tm_c, use_rope=False)
        z_x, qt_x, k_x, vt_x, gate_x = inproj(x, sx1, ax1, cos=cos, sin=sin,
                                              tm=tm_x, use_rope=True)
        attn_x = attn(qt_x, [(k_x, vt_x), (k_c, vt_c)], tq=tq_x)
        x = post(x, z_x, attn_x, gate_x, gx1, sx2, ax2, gx2, tm=tm_x)
        if not last:
            attn_c = attn(qt_c, [(k_c, vt_c)], tq=tq_c)
            h_ctx = post(h_ctx, z_c, attn_c, gate_c, gc1, sc2, ac2, gc2, tm=tm_c)
    return x
```

```python
import functools
import math

import jax
import jax.numpy as jnp
from jax import lax
from jax.experimental import pallas as pl
from jax.experimental.pallas import tpu as pltpu

D_MODEL = 1024
GRID_W = 64
CONV_WIDTH = 512
CONV_K = 31
N_HEADS = 8
QK_DIM = 64
V_DIM = 2 * QK_DIM
ATTN_WIDTH = N_HEADS * V_DIM
QK_COLS = N_HEADS * 2 * QK_DIM
D_FF = 4 * D_MODEL
ROPE_BASE = 10000.0
N_FREQ_AXIS = QK_DIM // 4
EPS = 1e-6
LOG2_E = math.log2(math.e)
SAFE_LOG2_SCORE = 60.0
SCORE_BOUND_MARGIN = 1.02

CONV_OFF = 0
Q_OFF = CONV_OFF + 2 * CONV_WIDTH
K_OFF = Q_OFF + QK_COLS
V_OFF = K_OFF + QK_COLS
GATE_OFF = V_OFF + ATTN_WIDTH
IN_COLS = GATE_OFF + 2 * D_MODEL

LANES = 128
SUBLANES = 8
BF16_ROWS = 16
HALO = 16
CONV_PAD = CONV_K // 2
ADA_ROWS = 24
VMEM_LIMIT = 56 * 1024 * 1024

BF16 = jnp.bfloat16
F32 = jnp.float32


def _const_spec(shape):
    zeros = (0,) * len(shape)
    return pl.BlockSpec(shape, lambda *_: zeros, pipeline_mode=pl.Buffered(1))


def _params(n_axes):
    return pltpu.CompilerParams(dimension_semantics=("parallel",) * n_axes,
                                vmem_limit_bytes=VMEM_LIMIT)


def _dot(a, b):
    return jnp.dot(a, b, preferred_element_type=F32)


def _sigmoid(t):
    return 1.0 / (1.0 + jnp.exp(-t))


def _norm_modulate(x, g, shift, scale):
    ms = jnp.mean(x * x, axis=-1, keepdims=True)
    h = x * lax.rsqrt(ms + EPS) * g
    return h * (1.0 + scale) + shift


def _ada_kernel(c_ref, w_ref, b_ref, o_ref):
    cv = c_ref[...]
    s = cv * _sigmoid(cv)
    o_ref[0] = _dot(s, w_ref[0]) + b_ref[0]


def _ada_call(cv, w_ada, b_ada):
    depth = w_ada.shape[0]
    n_col = 6
    return pl.pallas_call(
        _ada_kernel,
        grid=(depth, n_col),
        in_specs=[
            pl.BlockSpec((ADA_ROWS, D_MODEL), lambda i, j: (0, 0)),
            pl.BlockSpec((1, D_MODEL, D_MODEL), lambda i, j: (i, 0, j)),
            pl.BlockSpec((1, 1, D_MODEL), lambda i, j: (i, 0, j)),
        ],
        out_specs=pl.BlockSpec((1, ADA_ROWS, D_MODEL), lambda i, j: (i, 0, j)),
        out_shape=jax.ShapeDtypeStruct((depth, ADA_ROWS, n_col * D_MODEL), F32),
        compiler_params=_params(2),
        name="ada",
    )(cv, w_ada, b_ada.reshape(depth, 1, n_col * D_MODEL))


INPROJ_ROWS = 256


def _inproj_kernel(x_ref, shift_ref, scale_ref, g_ref, w_ref, bg_ref, qg_ref, kg_ref,
                   gsum_ref, cos_ref, sin_ref, *out_refs, use_rope, kv_only):
    if kv_only:
        k_ref, vt_ref = out_refs
    else:
        z_ref, qt_ref, k_ref, vt_ref, gate_ref = out_refs
    tm = x_ref.shape[1]
    group = min(INPROJ_ROWS, tm)

    lane = lax.broadcasted_iota(jnp.int32, (group, LANES), 1)
    first_half = (lane & (QK_DIM // 2)) == 0

    def qk_head(t, gain, rows):
        ms = _dot((t * t).astype(BF16), gsum_ref[...])
        t = t * lax.rsqrt(ms + EPS) * gain
        if use_rope:
            partner = jnp.where(first_half,
                                pltpu.roll(t, LANES - QK_DIM // 2, 1),
                                pltpu.roll(t, QK_DIM // 2, 1))
            t = t * cos_ref[rows, :] + partner * sin_ref[rows, :]
        return t

    for r0 in range(0, tm, group):
        rows = slice(r0, r0 + group)
        h = _norm_modulate(x_ref[0, rows, :], g_ref[...], shift_ref[0], scale_ref[0]).astype(BF16)

        if not kv_only:
            tg = _dot(h, w_ref[:, GATE_OFF:IN_COLS]) + bg_ref[...]
            gate_ref[0, rows, :] = _sigmoid(tg).astype(BF16)

            tq = _dot(h, w_ref[:, Q_OFF:K_OFF])
            for hd in range(N_HEADS):
                t = qk_head(tq[:, hd * LANES:(hd + 1) * LANES], qg_ref[...], rows)
                qt_ref[0, hd, :, rows] = t.T.astype(BF16)

        tk = _dot(h, w_ref[:, K_OFF:V_OFF])
        for hd in range(N_HEADS):
            t = qk_head(tk[:, hd * LANES:(hd + 1) * LANES], kg_ref[...], rows)
            k_ref[0, rows, hd * LANES:(hd + 1) * LANES] = t.astype(BF16)

        tv = _dot(h, w_ref[:, V_OFF:GATE_OFF])
        for hd in range(N_HEADS):
            vt_ref[0, hd, 0, :, rows] = tv[:, hd * LANES:(hd + 1) * LANES].T.astype(BF16)

        if not kv_only:
            u = _dot(h, w_ref[:, CONV_OFF:Q_OFF])
            z_ref[0, rows, :] = (u[:, :CONV_WIDTH] * _sigmoid(u[:, CONV_WIDTH:])).astype(BF16)


def _inproj_call(xs, shift, scale, g, w_in, b_gate, qg, kg, gsum, cos, sin, *,
                 tm, use_rope, kv_only=False):
    B, L, _ = xs.shape
    nt = L // tm
    row = lambda b, i: (b, i, 0)
    per_batch = pl.BlockSpec((1, 1, D_MODEL), lambda b, i: (b, 0, 0))
    rope_spec = pl.BlockSpec((tm, LANES), lambda b, i: (i, 0))
    k_spec = pl.BlockSpec((1, tm, QK_COLS), row)
    k_shape = jax.ShapeDtypeStruct((B, L, QK_COLS), BF16)
    vt_spec = pl.BlockSpec((1, N_HEADS, 1, V_DIM, tm), lambda b, i: (b, 0, i, 0, 0))
    vt_shape = jax.ShapeDtypeStruct((B, N_HEADS, nt, V_DIM, tm), BF16)
    if kv_only:
        out_specs, out_shape = [k_spec, vt_spec], [k_shape, vt_shape]
    else:
        out_specs = [
            pl.BlockSpec((1, tm, CONV_WIDTH), row),
            pl.BlockSpec((1, N_HEADS, V_DIM, tm), lambda b, i: (b, 0, 0, i)),
            k_spec, vt_spec,
            pl.BlockSpec((1, tm, 2 * D_MODEL), row),
        ]
        out_shape = [
            jax.ShapeDtypeStruct((B, L, CONV_WIDTH), BF16),
            jax.ShapeDtypeStruct((B, N_HEADS, V_DIM, L), BF16),
            k_shape, vt_shape,
            jax.ShapeDtypeStruct((B, L, 2 * D_MODEL), BF16),
        ]
    return pl.pallas_call(
        functools.partial(_inproj_kernel, use_rope=use_rope, kv_only=kv_only),
        grid=(B, nt),
        in_specs=[
            pl.BlockSpec((1, tm, D_MODEL), row),
            per_batch, per_batch,
            _const_spec((1, D_MODEL)),
            _const_spec((D_MODEL, IN_COLS)),
            _const_spec((1, 2 * D_MODEL)),
            _const_spec((1, LANES)), _const_spec((1, LANES)),
            _const_spec((LANES, LANES)),
            rope_spec, rope_spec,
        ],
        out_specs=out_specs,
        out_shape=out_shape,
        compiler_params=_params(2),
        name="inproj",
    )(xs, shift, scale, g, w_in, b_gate, qg, kg, gsum, cos, sin)


def _attn_kernel(*refs, n_src, lam_init, bounded):
    qt_ref = refs[0]
    src_refs = refs[1:1 + 2 * n_src]
    lq_ref, lk_ref, ng_ref, o_ref = refs[1 + 2 * n_src:]
    heads, tq = qt_ref.shape[1], qt_ref.shape[3]

    lq = lq_ref[...]
    lk = lk_ref[...]
    dots = jnp.sum(lq * lk, axis=-1, keepdims=True)
    e = jnp.exp(dots)
    lam = e[0:1] - e[1:2] + lam_init
    gain = jnp.tile(ng_ref[...], (1, tq // LANES))

    chunks = []
    for si in range(n_src):
        k_ref, vt_ref = src_refs[2 * si], src_refs[2 * si + 1]
        n_chunks, tk = vt_ref.shape[2], vt_ref.shape[4]
        chunks += [(k_ref, vt_ref, ci, tk) for ci in range(n_chunks)]

    for hd in range(heads):
        qt = qt_ref[0, hd]
        sub = lax.broadcasted_iota(jnp.int32, qt.shape, 0)
        zero = jnp.zeros_like(qt)
        qm = jnp.concatenate([jnp.where(sub < QK_DIM, qt, zero),
                              jnp.where(sub >= QK_DIM, qt, zero)], axis=1)

        def scores(chunk, qm=qm, hd=hd):
            k_ref, _, ci, tk = chunk
            return _dot(k_ref[0, ci * tk:(ci + 1) * tk, hd * LANES:(hd + 1) * LANES], qm)

        m = None
        acc = None
        s_next = scores(chunks[0])
        for idx, (_, vt_ref, ci, tk) in enumerate(chunks):
            s = s_next
            if idx + 1 < len(chunks):
                s_next = scores(chunks[idx + 1])
            ones_rows = (lax.broadcasted_iota(jnp.int32, (BF16_ROWS, tk), 0) == 0).astype(BF16)
            vtc = jnp.concatenate([vt_ref[0, hd, ci], ones_rows], axis=0)
            if bounded:
                pv = _dot(vtc, jnp.exp2(s).astype(BF16))
                acc = pv if acc is None else acc + pv
            else:
                s_max = jnp.max(s, axis=0, keepdims=True)
                m_new = s_max if m is None else jnp.maximum(m, s_max)
                pv = _dot(vtc, jnp.exp2(s - m_new).astype(BF16))
                acc = pv if acc is None else jnp.exp2(m - m_new) * acc + pv
                m = m_new

        inv = 1.0 / acc[V_DIM:V_DIM + 1, :]
        o = acc[:V_DIM, :tq] * inv[:, :tq] - lam * (acc[:V_DIM, tq:] * inv[:, tq:])
        ms = jnp.mean(o * o, axis=0, keepdims=True)
        o = o * lax.rsqrt(ms + EPS) * gain * (1.0 - lam_init)
        o_ref[0, hd] = o.astype(BF16)


def _attn_call(qt, sources, lam_q, lam_k, norm_g, *, tq, heads, lam_init, bounded):
    B, _, _, L = qt.shape
    in_specs = [pl.BlockSpec((1, heads, V_DIM, tq), lambda b, h, i: (b, h, 0, i))]
    args = [qt]
    for k, vt in sources:
        T = k.shape[1]
        n_chunks, tk = vt.shape[2], vt.shape[4]
        in_specs.append(pl.BlockSpec((1, T, heads * LANES), lambda b, h, i: (b, 0, h)))
        in_specs.append(pl.BlockSpec((1, heads, n_chunks, V_DIM, tk),
                                     lambda b, h, i: (b, h, 0, 0, 0)))
        args += [k, vt]
    in_specs += [_const_spec((2, QK_DIM)), _const_spec((2, QK_DIM)), _const_spec((V_DIM, LANES))]
    args += [lam_q, lam_k, norm_g]
    return pl.pallas_call(
        functools.partial(_attn_kernel, n_src=len(sources), lam_init=lam_init, bounded=bounded),
        grid=(B, N_HEADS // heads, L // tq),
        in_specs=in_specs,
        out_specs=pl.BlockSpec((1, heads, V_DIM, tq), lambda b, h, i: (b, h, 0, i)),
        out_shape=jax.ShapeDtypeStruct((B, N_HEADS, V_DIM, L), BF16),
        compiler_params=_params(3),
        name="attn",
    )(*args)


CONV_ROWS = 32
FF_CHUNK = 1024
MERGE_ROWS = 256


def _post_kernel(x_ref, z_ref, zprev_ref, znext_ref, attn_ref, gate_ref, g1_ref,
                 shift2_ref, scale2_ref, g2_ref,
                 wdw_ref, bdw_ref, lng_ref, lnb_ref, wco_ref, wao_ref, wo_ref,
                 n2_ref, w1_ref, w2_ref, o_ref, zext_ref):
    i = pl.program_id(1)
    n = pl.num_programs(1)
    tm = z_ref.shape[1]

    prev = zprev_ref[0].astype(F32)
    nxt = znext_ref[0].astype(F32)
    zext_ref[0, 0:HALO, :] = jnp.where(i > 0, prev, 0.0)
    zext_ref[0, HALO:HALO + tm, :] = z_ref[0].astype(F32)
    zext_ref[0, HALO + tm:, :] = jnp.where(i < n - 1, nxt, 0.0)
    n_shifted = tm + 2 * HALO - SUBLANES
    for b in range(1, SUBLANES):
        zext_ref[b, 0:n_shifted, :] = zext_ref[0, b:b + n_shifted, :]

    def conv_act(base):
        cv = jnp.broadcast_to(bdw_ref[...], (CONV_ROWS // SUBLANES, SUBLANES, CONV_WIDTH))
        for k in range(CONV_K):
            a, b = divmod(HALO - CONV_PAD + k, SUBLANES)
            start = base + a * SUBLANES
            zk = zext_ref[b, start:start + CONV_ROWS, :]
            cv = cv + zk.reshape(CONV_ROWS // SUBLANES, SUBLANES, CONV_WIDTH) * wdw_ref[k][None]
        cv = cv.reshape(CONV_ROWS, CONV_WIDTH)
        mu = jnp.mean(cv, axis=-1, keepdims=True)
        cc = cv - mu
        var = jnp.mean(cc * cc, axis=-1, keepdims=True)
        y = cc * lax.rsqrt(var + EPS) * lng_ref[...] + lnb_ref[...]
        return (y * _sigmoid(y)).astype(BF16)

    group = min(MERGE_ROWS, tm)
    for r0 in range(0, tm, group):
        rows = slice(r0, r0 + group)
        yact = jnp.concatenate([conv_act(r0 + c) for c in range(0, group, CONV_ROWS)], axis=0)
        y_conv = _dot(yact, wco_ref[...])
        attn_rows = jnp.concatenate([attn_ref[0, hd, :, rows].T for hd in range(N_HEADS)], axis=1)
        y_attn = _dot(attn_rows, wao_ref[...])
        gates = gate_ref[0, rows, :].astype(F32)
        merged = gates[:, :D_MODEL] * y_conv + gates[:, D_MODEL:] * y_attn
        y_out = _dot(merged.astype(BF16), wo_ref[...])
        x1 = x_ref[0, rows, :] + g1_ref[0] * y_out
        h = _norm_modulate(x1, n2_ref[...], shift2_ref[0], scale2_ref[0]).astype(BF16)
        y = jnp.zeros(x1.shape, F32)
        for c in range(D_FF // FF_CHUNK):
            u = jnp.maximum(_dot(h, w1_ref[:, c * FF_CHUNK:(c + 1) * FF_CHUNK]), 0.0)
            y = y + _dot((u * u).astype(BF16), w2_ref[c * FF_CHUNK:(c + 1) * FF_CHUNK, :])
        o_ref[0, rows, :] = x1 + g2_ref[0] * y


def _post_call(xs, z, attn, gates, g1, shift2, scale2, g2, w_dw, b_dw, ln_g, ln_b, w_co, w_ao,
               w_o, n2, w1, w2, *, tm):
    B, L, _ = xs.shape
    nt = L // tm
    hb = tm // HALO
    last_hb = L // HALO - 1
    row = lambda b, i: (b, i, 0)
    per_batch = pl.BlockSpec((1, 1, D_MODEL), lambda b, i: (b, 0, 0))
    return pl.pallas_call(
        _post_kernel,
        grid=(B, nt),
        in_specs=[
            pl.BlockSpec((1, tm, D_MODEL), row),
            pl.BlockSpec((1, tm, CONV_WIDTH), row),
            pl.BlockSpec((1, HALO, CONV_WIDTH), lambda b, i: (b, jnp.maximum(i * hb - 1, 0), 0)),
            pl.BlockSpec((1, HALO, CONV_WIDTH), lambda b, i: (b, jnp.minimum((i + 1) * hb, last_hb), 0)),
            pl.BlockSpec((1, N_HEADS, V_DIM, tm), lambda b, i: (b, 0, 0, i)),
            pl.BlockSpec((1, tm, 2 * D_MODEL), row),
            per_batch, per_batch, per_batch, per_batch,
            _const_spec((CONV_K, SUBLANES, CONV_WIDTH)),
            _const_spec((1, CONV_WIDTH)), _const_spec((1, CONV_WIDTH)), _const_spec((1, CONV_WIDTH)),
            _const_spec((CONV_WIDTH, D_MODEL)),
            _const_spec((ATTN_WIDTH, D_MODEL)),
            _const_spec((D_MODEL, D_MODEL)),
            _const_spec((1, D_MODEL)),
            _const_spec((D_MODEL, D_FF)),
            _const_spec((D_FF, D_MODEL)),
        ],
        out_specs=pl.BlockSpec((1, tm, D_MODEL), row),
        out_shape=jax.ShapeDtypeStruct((B, L, D_MODEL), F32),
        scratch_shapes=[pltpu.VMEM((SUBLANES, tm + 2 * HALO, CONV_WIDTH), F32)],
        compiler_params=_params(2),
        name="post",
    )(xs, z, z, z, attn, gates, g1, shift2, scale2, g2, w_dw, b_dw, ln_g, ln_b, w_co, w_ao, w_o,
      n2, w1, w2)


def _rope_tables(n_tokens):
    rows = n_tokens // GRID_W
    row = jnp.repeat(jnp.arange(rows, dtype=F32), GRID_W)
    col = jnp.tile(jnp.arange(GRID_W, dtype=F32), rows)
    inv = jnp.power(ROPE_BASE, -jnp.arange(N_FREQ_AXIS, dtype=F32) / N_FREQ_AXIS)
    ang = jnp.concatenate([row[:, None] * inv, col[:, None] * inv], axis=-1)
    cos, sin = jnp.cos(ang), jnp.sin(ang)
    cos = jnp.tile(cos, (1, 4))
    sin = jnp.tile(jnp.concatenate([-sin, sin], axis=-1), (1, 2))
    return cos, sin


def _tiles(seq_len, ctx_len):
    latent = dict(tm=min(512, seq_len), tq=min(1024, seq_len))
    context = dict(tm=ctx_len, tq=ctx_len)
    return latent, context


def kernel(x, c, ctx, c_ctx, w_ada, b_ada, norm1_g, w_in, b_gate, q_norm_g, k_norm_g,
           lam_q, lam_k, attn_norm_g, w_dw, b_dw, conv_ln_g, conv_ln_b, w_conv_out,
           w_attn_out, w_out, norm2_g, w_mlp1, w_mlp2):
    B, S, _ = x.shape
    C = ctx.shape[1]
    depth = w_ada.shape[0]
    tile_x, tile_c = _tiles(S, C)
    tm_x, tq_x = tile_x["tm"], tile_x["tq"]
    tm_c, tq_c = tile_c["tm"], tile_c["tq"]

    cv = jnp.concatenate([c, c_ctx[None, :], jnp.zeros((ADA_ROWS - B - 1, D_MODEL), F32)], axis=0)
    mod = _ada_call(cv, w_ada, b_ada)

    cos, sin = _rope_tables(S)
    cos_c = jnp.ones((C, LANES), F32)
    sin_c = jnp.zeros((C, LANES), F32)
    grp = jnp.arange(LANES) // QK_DIM
    gsum = jnp.where(grp[:, None] == grp[None, :], 1.0 / QK_DIM, 0.0).astype(BF16)
    qk_scale = QK_DIM ** -0.5

    h_ctx = ctx
    for i in range(depth):
        last = i == depth - 1
        lam_init = 0.8 - 0.6 * math.exp(-0.3 * i)
        mx = [t.reshape(B, 1, D_MODEL) for t in jnp.split(mod[i, :B], 6, axis=-1)]
        mc = [jnp.broadcast_to(t.reshape(1, 1, D_MODEL), (B, 1, D_MODEL))
              for t in jnp.split(mod[i, B], 6, axis=-1)]
        sx1, ax1, gx1, sx2, ax2, gx2 = mx
        sc1, ac1, gc1, sc2, ac2, gc2 = mc

        w_in_i = w_in[i].astype(BF16)
        bg = b_gate[i].reshape(1, 2 * D_MODEL)
        g1 = norm1_g[i].reshape(1, D_MODEL)
        qg = jnp.tile(q_norm_g[i], 2).reshape(1, LANES) * (qk_scale * LOG2_E)
        kg = jnp.tile(k_norm_g[i], 2).reshape(1, LANES)
        inproj = functools.partial(_inproj_call, g=g1, w_in=w_in_i, b_gate=bg, qg=qg, kg=kg,
                                   gsum=gsum)
        post = functools.partial(
            _post_call,
            w_dw=jnp.broadcast_to(w_dw[i][:, None, :], (CONV_K, SUBLANES, CONV_WIDTH)),
            b_dw=b_dw[i].reshape(1, -1),
            ln_g=conv_ln_g[i].reshape(1, -1), ln_b=conv_ln_b[i].reshape(1, -1),
            w_co=w_conv_out[i].astype(BF16), w_ao=w_attn_out[i].astype(BF16),
            w_o=w_out[i].astype(BF16), n2=norm2_g[i].reshape(1, D_MODEL),
            w1=w_mlp1[i].astype(BF16), w2=w_mlp2[i].astype(BF16))
        attn_any = functools.partial(_attn_call, lam_q=lam_q[i], lam_k=lam_k[i],
                                     norm_g=jnp.broadcast_to(attn_norm_g[i][:, None], (V_DIM, LANES)),
                                     lam_init=lam_init)
        score_bound = QK_DIM * jnp.max(jnp.abs(qg)) * jnp.max(jnp.abs(kg)) * SCORE_BOUND_MARGIN

        def attn(qt, sources, *, tq, heads, attn_any=attn_any, score_bound=score_bound):
            return lax.cond(score_bound <= SAFE_LOG2_SCORE,
                            lambda: attn_any(qt, sources, tq=tq, heads=heads, bounded=True),
                            lambda: attn_any(qt, sources, tq=tq, heads=heads, bounded=False))

        if last:
            k_c, vt_c = inproj(h_ctx, sc1, ac1, cos=cos_c, sin=sin_c, tm=tm_c, use_rope=False,
                               kv_only=True)
        else:
            z_c, qt_c, k_c, vt_c, gate_c = inproj(h_ctx, sc1, ac1, cos=cos_c, sin=sin_c,
                                                  tm=tm_c, use_rope=False)
        z_x, qt_x, k_x, vt_x, gate_x = inproj(x, sx1, ax1, cos=cos, sin=sin,
                                              tm=tm_x, use_rope=True)
        attn_x = attn(qt_x, [(k_x, vt_x), (k_c, vt_c)], tq=tq_x, heads=1)
        x = post(x, z_x, attn_x, gate_x, gx1, sx2, ax2, gx2, tm=tm_x)
        if not last:
            attn_c = attn(qt_c, [(k_c, vt_c)], tq=tq_c, heads=N_HEADS)
            h_ctx = post(h_ctx, z_c, attn_c, gate_c, gc1, sc2, ac2, gc2, tm=tm_c)
    return x
```

```python
import functools
import math

import jax
import jax.numpy as jnp
from jax import lax
from jax.experimental import pallas as pl
from jax.experimental.pallas import tpu as pltpu

D_MODEL = 1024
GRID_W = 64
CONV_WIDTH = 512
CONV_K = 31
N_HEADS = 8
QK_DIM = 64
V_DIM = 2 * QK_DIM
ATTN_WIDTH = N_HEADS * V_DIM
QK_COLS = N_HEADS * 2 * QK_DIM
D_FF = 4 * D_MODEL
ROPE_BASE = 10000.0
N_FREQ_AXIS = QK_DIM // 4
EPS = 1e-6
LOG2_E = math.log2(math.e)
SAFE_LOG2_SCORE = 60.0
SCORE_BOUND_MARGIN = 1.02

CONV_OFF = 0
Q_OFF = CONV_OFF + 2 * CONV_WIDTH
K_OFF = Q_OFF + QK_COLS
V_OFF = K_OFF + QK_COLS
GATE_OFF = V_OFF + ATTN_WIDTH
IN_COLS = GATE_OFF + 2 * D_MODEL

LANES = 128
SUBLANES = 8
BF16_ROWS = 16
HALO = 16
CONV_PAD = CONV_K // 2
ADA_ROWS = 24
VMEM_LIMIT = 56 * 1024 * 1024

BF16 = jnp.bfloat16
F32 = jnp.float32


def _const_spec(shape):
    zeros = (0,) * len(shape)
    return pl.BlockSpec(shape, lambda *_: zeros, pipeline_mode=pl.Buffered(1))


def _params(n_axes):
    return pltpu.CompilerParams(dimension_semantics=("parallel",) * n_axes,
                                vmem_limit_bytes=VMEM_LIMIT)


def _dot(a, b):
    return jnp.dot(a, b, preferred_element_type=F32)


def _sigmoid(t):
    return 1.0 / (1.0 + jnp.exp(-t))


def _norm_modulate(x, g, shift, scale):
    ms = jnp.mean(x * x, axis=-1, keepdims=True)
    h = x * lax.rsqrt(ms + EPS) * g
    return h * (1.0 + scale) + shift


def _ada_kernel(c_ref, w_ref, b_ref, o_ref):
    cv = c_ref[...]
    s = cv * _sigmoid(cv)
    o_ref[0] = _dot(s, w_ref[0]) + b_ref[0]


def _ada_call(cv, w_ada, b_ada):
    depth = w_ada.shape[0]
    n_col = 6
    return pl.pallas_call(
        _ada_kernel,
        grid=(depth, n_col),
        in_specs=[
            pl.BlockSpec((ADA_ROWS, D_MODEL), lambda i, j: (0, 0)),
            pl.BlockSpec((1, D_MODEL, D_MODEL), lambda i, j: (i, 0, j)),
            pl.BlockSpec((1, 1, D_MODEL), lambda i, j: (i, 0, j)),
        ],
        out_specs=pl.BlockSpec((1, ADA_ROWS, D_MODEL), lambda i, j: (i, 0, j)),
        out_shape=jax.ShapeDtypeStruct((depth, ADA_ROWS, n_col * D_MODEL), F32),
        compiler_params=_params(2),
        name="ada",
    )(cv, w_ada, b_ada.reshape(depth, 1, n_col * D_MODEL))


def _inproj_kernel(x_ref, shift_ref, scale_ref, g_ref, w_ref, bg_ref, qg_ref, kg_ref,
                   gsum_ref, cos_ref, sin_ref, *out_refs, use_rope, kv_only):
    if kv_only:
        k_ref, vt_ref = out_refs
    else:
        z_ref, qt_ref, k_ref, vt_ref, gate_ref = out_refs
    h = _norm_modulate(x_ref[0], g_ref[...], shift_ref[0], scale_ref[0]).astype(BF16)
    tm = h.shape[0]

    lane = lax.broadcasted_iota(jnp.int32, (tm, LANES), 1)
    first_half = (lane & (QK_DIM // 2)) == 0

    def qk_head(t, gain):
        ms = _dot((t * t).astype(BF16), gsum_ref[...])
        t = t * lax.rsqrt(ms + EPS) * gain
        if use_rope:
            partner = jnp.where(first_half,
                                pltpu.roll(t, LANES - QK_DIM // 2, 1),
                                pltpu.roll(t, QK_DIM // 2, 1))
            t = t * cos_ref[...] + partner * sin_ref[...]
        return t

    if not kv_only:
        u = _dot(h, w_ref[:, CONV_OFF:Q_OFF])
        z_ref[0] = (u[:, :CONV_WIDTH] * _sigmoid(u[:, CONV_WIDTH:])).astype(BF16)

        tq = _dot(h, w_ref[:, Q_OFF:K_OFF])
        for hd in range(N_HEADS):
            t = qk_head(tq[:, hd * LANES:(hd + 1) * LANES], qg_ref[...])
            qt_ref[0, hd] = t.T.astype(BF16)

    tk = _dot(h, w_ref[:, K_OFF:V_OFF])
    for hd in range(N_HEADS):
        t = qk_head(tk[:, hd * LANES:(hd + 1) * LANES], kg_ref[...])
        k_ref[0, :, hd * LANES:(hd + 1) * LANES] = t.astype(BF16)

    tv = _dot(h, w_ref[:, V_OFF:GATE_OFF])
    for hd in range(N_HEADS):
        vt_ref[0, hd, 0] = tv[:, hd * LANES:(hd + 1) * LANES].T.astype(BF16)

    if not kv_only:
        tg = _dot(h, w_ref[:, GATE_OFF:IN_COLS]) + bg_ref[...]
        gate_ref[0] = _sigmoid(tg).astype(BF16)


def _inproj_call(xs, shift, scale, g, w_in, b_gate, qg, kg, gsum, cos, sin, *,
                 tm, use_rope, kv_only=False):
    B, L, _ = xs.shape
    nt = L // tm
    row = lambda b, i: (b, i, 0)
    per_batch = pl.BlockSpec((1, 1, D_MODEL), lambda b, i: (b, 0, 0))
    rope_spec = pl.BlockSpec((tm, LANES), lambda b, i: (i, 0))
    k_spec = pl.BlockSpec((1, tm, QK_COLS), row)
    k_shape = jax.ShapeDtypeStruct((B, L, QK_COLS), BF16)
    vt_spec = pl.BlockSpec((1, N_HEADS, 1, V_DIM, tm), lambda b, i: (b, 0, i, 0, 0))
    vt_shape = jax.ShapeDtypeStruct((B, N_HEADS, nt, V_DIM, tm), BF16)
    if kv_only:
        out_specs, out_shape = [k_spec, vt_spec], [k_shape, vt_shape]
    else:
        out_specs = [
            pl.BlockSpec((1, tm, CONV_WIDTH), row),
            pl.BlockSpec((1, N_HEADS, V_DIM, tm), lambda b, i: (b, 0, 0, i)),
            k_spec, vt_spec,
            pl.BlockSpec((1, tm, 2 * D_MODEL), row),
        ]
        out_shape = [
            jax.ShapeDtypeStruct((B, L, CONV_WIDTH), BF16),
            jax.ShapeDtypeStruct((B, N_HEADS, V_DIM, L), BF16),
            k_shape, vt_shape,
            jax.ShapeDtypeStruct((B, L, 2 * D_MODEL), BF16),
        ]
    return pl.pallas_call(
        functools.partial(_inproj_kernel, use_rope=use_rope, kv_only=kv_only),
        grid=(B, nt),
        in_specs=[
            pl.BlockSpec((1, tm, D_MODEL), row),
            per_batch, per_batch,
            _const_spec((1, D_MODEL)),
            _const_spec((D_MODEL, IN_COLS)),
            _const_spec((1, 2 * D_MODEL)),
            _const_spec((1, LANES)), _const_spec((1, LANES)),
            _const_spec((LANES, LANES)),
            rope_spec, rope_spec,
        ],
        out_specs=out_specs,
        out_shape=out_shape,
        compiler_params=_params(2),
        name="inproj",
    )(xs, shift, scale, g, w_in, b_gate, qg, kg, gsum, cos, sin)


def _attn_kernel(*refs, n_src, lam_init, bounded):
    qt_ref = refs[0]
    src_refs = refs[1:1 + 2 * n_src]
    lq_ref, lk_ref, ng_ref, o_ref = refs[1 + 2 * n_src:]
    heads, tq = qt_ref.shape[1], qt_ref.shape[3]

    lq = lq_ref[...]
    lk = lk_ref[...]
    dots = jnp.sum(lq * lk, axis=-1, keepdims=True)
    e = jnp.exp(dots)
    lam = e[0:1] - e[1:2] + lam_init
    gain = jnp.tile(ng_ref[...], (1, tq // LANES))

    chunks = []
    for si in range(n_src):
        k_ref, vt_ref = src_refs[2 * si], src_refs[2 * si + 1]
        n_chunks, tk = vt_ref.shape[2], vt_ref.shape[4]
        chunks += [(k_ref, vt_ref, ci, tk) for ci in range(n_chunks)]

    for hd in range(heads):
        qt = qt_ref[0, hd]
        sub = lax.broadcasted_iota(jnp.int32, qt.shape, 0)
        zero = jnp.zeros_like(qt)
        qm = jnp.concatenate([jnp.where(sub < QK_DIM, qt, zero),
                              jnp.where(sub >= QK_DIM, qt, zero)], axis=1)

        def scores(chunk, qm=qm, hd=hd):
            k_ref, _, ci, tk = chunk
            return _dot(k_ref[0, ci * tk:(ci + 1) * tk, hd * LANES:(hd + 1) * LANES], qm)

        m = None
        acc = None
        s_next = scores(chunks[0])
        for idx, (_, vt_ref, ci, tk) in enumerate(chunks):
            s = s_next
            if idx + 1 < len(chunks):
                s_next = scores(chunks[idx + 1])
            ones_rows = (lax.broadcasted_iota(jnp.int32, (BF16_ROWS, tk), 0) == 0).astype(BF16)
            vtc = jnp.concatenate([vt_ref[0, hd, ci], ones_rows], axis=0)
            if bounded:
                pv = _dot(vtc, jnp.exp2(s).astype(BF16))
                acc = pv if acc is None else acc + pv
            else:
                s_max = jnp.max(s, axis=0, keepdims=True)
                m_new = s_max if m is None else jnp.maximum(m, s_max)
                pv = _dot(vtc, jnp.exp2(s - m_new).astype(BF16))
                acc = pv if acc is None else jnp.exp2(m - m_new) * acc + pv
                m = m_new

        inv = 1.0 / acc[V_DIM:V_DIM + 1, :]
        o = acc[:V_DIM, :tq] * inv[:, :tq] - lam * (acc[:V_DIM, tq:] * inv[:, tq:])
        ms = jnp.mean(o * o, axis=0, keepdims=True)
        o = o * lax.rsqrt(ms + EPS) * gain * (1.0 - lam_init)
        o_ref[0, hd] = o.astype(BF16)


def _attn_call(qt, sources, lam_q, lam_k, norm_g, *, tq, heads, lam_init, bounded):
    B, _, _, L = qt.shape
    in_specs = [pl.BlockSpec((1, heads, V_DIM, tq), lambda b, h, i: (b, h, 0, i))]
    args = [qt]
    for k, vt in sources:
        T = k.shape[1]
        n_chunks, tk = vt.shape[2], vt.shape[4]
        in_specs.append(pl.BlockSpec((1, T, heads * LANES), lambda b, h, i: (b, 0, h)))
        in_specs.append(pl.BlockSpec((1, heads, n_chunks, V_DIM, tk),
                                     lambda b, h, i: (b, h, 0, 0, 0)))
        args += [k, vt]
    in_specs += [_const_spec((2, QK_DIM)), _const_spec((2, QK_DIM)), _const_spec((V_DIM, LANES))]
    args += [lam_q, lam_k, norm_g]
    return pl.pallas_call(
        functools.partial(_attn_kernel, n_src=len(sources), lam_init=lam_init, bounded=bounded),
        grid=(B, N_HEADS // heads, L // tq),
        in_specs=in_specs,
        out_specs=pl.BlockSpec((1, heads, V_DIM, tq), lambda b, h, i: (b, h, 0, i)),
        out_shape=jax.ShapeDtypeStruct((B, N_HEADS, V_DIM, L), BF16),
        compiler_params=_params(3),
        name="attn",
    )(*args)


CONV_ROWS = 32
FF_CHUNK = 1024
MERGE_ROWS = 256


def _post_kernel(x_ref, z_ref, zprev_ref, znext_ref, attn_ref, gate_ref, g1_ref,
                 shift2_ref, scale2_ref, g2_ref,
                 wdw_ref, bdw_ref, lng_ref, lnb_ref, wco_ref, wao_ref, wo_ref,
                 n2_ref, w1_ref, w2_ref, o_ref, zext_ref):
    i = pl.program_id(1)
    n = pl.num_programs(1)
    tm = z_ref.shape[1]

    prev = zprev_ref[0].astype(F32)
    nxt = znext_ref[0].astype(F32)
    zext_ref[0, 0:HALO, :] = jnp.where(i > 0, prev, 0.0)
    zext_ref[0, HALO:HALO + tm, :] = z_ref[0].astype(F32)
    zext_ref[0, HALO + tm:, :] = jnp.where(i < n - 1, nxt, 0.0)
    n_shifted = tm + 2 * HALO - SUBLANES
    for b in range(1, SUBLANES):
        zext_ref[b, 0:n_shifted, :] = zext_ref[0, b:b + n_shifted, :]

    def conv_act(base):
        cv = jnp.broadcast_to(bdw_ref[...], (CONV_ROWS // SUBLANES, SUBLANES, CONV_WIDTH))
        for k in range(CONV_K):
            a, b = divmod(HALO - CONV_PAD + k, SUBLANES)
            start = base + a * SUBLANES
            zk = zext_ref[b, start:start + CONV_ROWS, :]
            cv = cv + zk.reshape(CONV_ROWS // SUBLANES, SUBLANES, CONV_WIDTH) * wdw_ref[k][None]
        cv = cv.reshape(CONV_ROWS, CONV_WIDTH)
        mu = jnp.mean(cv, axis=-1, keepdims=True)
        cc = cv - mu
        var = jnp.mean(cc * cc, axis=-1, keepdims=True)
        y = cc * lax.rsqrt(var + EPS) * lng_ref[...] + lnb_ref[...]
        return (y * _sigmoid(y)).astype(BF16)

    group = min(MERGE_ROWS, tm)
    for r0 in range(0, tm, group):
        rows = slice(r0, r0 + group)
        yact = jnp.concatenate([conv_act(r0 + c) for c in range(0, group, CONV_ROWS)], axis=0)
        y_conv = _dot(yact, wco_ref[...])
        attn_rows = jnp.concatenate([attn_ref[0, hd, :, rows].T for hd in range(N_HEADS)], axis=1)
        y_attn = _dot(attn_rows, wao_ref[...])
        gates = gate_ref[0, rows, :].astype(F32)
        merged = gates[:, :D_MODEL] * y_conv + gates[:, D_MODEL:] * y_attn
        y_out = _dot(merged.astype(BF16), wo_ref[...])
        x1 = x_ref[0, rows, :] + g1_ref[0] * y_out
        h = _norm_modulate(x1, n2_ref[...], shift2_ref[0], scale2_ref[0]).astype(BF16)
        y = jnp.zeros(x1.shape, F32)
        for c in range(D_FF // FF_CHUNK):
            u = jnp.maximum(_dot(h, w1_ref[:, c * FF_CHUNK:(c + 1) * FF_CHUNK]), 0.0)
            y = y + _dot((u * u).astype(BF16), w2_ref[c * FF_CHUNK:(c + 1) * FF_CHUNK, :])
        o_ref[0, rows, :] = x1 + g2_ref[0] * y


def _post_call(xs, z, attn, gates, g1, shift2, scale2, g2, w_dw, b_dw, ln_g, ln_b, w_co, w_ao,
               w_o, n2, w1, w2, *, tm):
    B, L, _ = xs.shape
    nt = L // tm
    hb = tm // HALO
    last_hb = L // HALO - 1
    row = lambda b, i: (b, i, 0)
    per_batch = pl.BlockSpec((1, 1, D_MODEL), lambda b, i: (b, 0, 0))
    return pl.pallas_call(
        _post_kernel,
        grid=(B, nt),
        in_specs=[
            pl.BlockSpec((1, tm, D_MODEL), row),
            pl.BlockSpec((1, tm, CONV_WIDTH), row),
            pl.BlockSpec((1, HALO, CONV_WIDTH), lambda b, i: (b, jnp.maximum(i * hb - 1, 0), 0)),
            pl.BlockSpec((1, HALO, CONV_WIDTH), lambda b, i: (b, jnp.minimum((i + 1) * hb, last_hb), 0)),
            pl.BlockSpec((1, N_HEADS, V_DIM, tm), lambda b, i: (b, 0, 0, i)),
            pl.BlockSpec((1, tm, 2 * D_MODEL), row),
            per_batch, per_batch, per_batch, per_batch,
            _const_spec((CONV_K, SUBLANES, CONV_WIDTH)),
            _const_spec((1, CONV_WIDTH)), _const_spec((1, CONV_WIDTH)), _const_spec((1, CONV_WIDTH)),
            _const_spec((CONV_WIDTH, D_MODEL)),
            _const_spec((ATTN_WIDTH, D_MODEL)),
            _const_spec((D_MODEL, D_MODEL)),
            _const_spec((1, D_MODEL)),
            _const_spec((D_MODEL, D_FF)),
            _const_spec((D_FF, D_MODEL)),
        ],
        out_specs=pl.BlockSpec((1, tm, D_MODEL), row),
        out_shape=jax.ShapeDtypeStruct((B, L, D_MODEL), F32),
        scratch_shapes=[pltpu.VMEM((SUBLANES, tm + 2 * HALO, CONV_WIDTH), F32)],
        compiler_params=_params(2),
        name="post",
    )(xs, z, z, z, attn, gates, g1, shift2, scale2, g2, w_dw, b_dw, ln_g, ln_b, w_co, w_ao, w_o,
      n2, w1, w2)


def _rope_tables(n_tokens):
    rows = n_tokens // GRID_W
    row = jnp.repeat(jnp.arange(rows, dtype=F32), GRID_W)
    col = jnp.tile(jnp.arange(GRID_W, dtype=F32), rows)
    inv = jnp.power(ROPE_BASE, -jnp.arange(N_FREQ_AXIS, dtype=F32) / N_FREQ_AXIS)
    ang = jnp.concatenate([row[:, None] * inv, col[:, None] * inv], axis=-1)
    cos, sin = jnp.cos(ang), jnp.sin(ang)
    cos = jnp.tile(cos, (1, 4))
    sin = jnp.tile(jnp.concatenate([-sin, sin], axis=-1), (1, 2))
    return cos, sin


def _tiles(seq_len, ctx_len):
    latent = dict(tm=min(512, seq_len), tq=min(2048, seq_len))
    context = dict(tm=ctx_len, tq=ctx_len)
    return latent, context


def kernel(x, c, ctx, c_ctx, w_ada, b_ada, norm1_g, w_in, b_gate, q_norm_g, k_norm_g,
           lam_q, lam_k, attn_norm_g, w_dw, b_dw, conv_ln_g, conv_ln_b, w_conv_out,
           w_attn_out, w_out, norm2_g, w_mlp1, w_mlp2):
    B, S, _ = x.shape
    C = ctx.shape[1]
    depth = w_ada.shape[0]
    tile_x, tile_c = _tiles(S, C)
    tm_x, tq_x = tile_x["tm"], tile_x["tq"]
    tm_c, tq_c = tile_c["tm"], tile_c["tq"]

    cv = jnp.concatenate([c, c_ctx[None, :], jnp.zeros((ADA_ROWS - B - 1, D_MODEL), F32)], axis=0)
    mod = _ada_call(cv, w_ada, b_ada)

    cos, sin = _rope_tables(S)
    cos_c = jnp.ones((C, LANES), F32)
    sin_c = jnp.zeros((C, LANES), F32)
    grp = jnp.arange(LANES) // QK_DIM
    gsum = jnp.where(grp[:, None] == grp[None, :], 1.0 / QK_DIM, 0.0).astype(BF16)
    qk_scale = QK_DIM ** -0.5

    h_ctx = ctx
    for i in range(depth):
        last = i == depth - 1
        lam_init = 0.8 - 0.6 * math.exp(-0.3 * i)
        mx = [t.reshape(B, 1, D_MODEL) for t in jnp.split(mod[i, :B], 6, axis=-1)]
        mc = [jnp.broadcast_to(t.reshape(1, 1, D_MODEL), (B, 1, D_MODEL))
              for t in jnp.split(mod[i, B], 6, axis=-1)]
        sx1, ax1, gx1, sx2, ax2, gx2 = mx
        sc1, ac1, gc1, sc2, ac2, gc2 = mc

        w_in_i = w_in[i].astype(BF16)
        bg = b_gate[i].reshape(1, 2 * D_MODEL)
        g1 = norm1_g[i].reshape(1, D_MODEL)
        qg = jnp.tile(q_norm_g[i], 2).reshape(1, LANES) * (qk_scale * LOG2_E)
        kg = jnp.tile(k_norm_g[i], 2).reshape(1, LANES)
        inproj = functools.partial(_inproj_call, g=g1, w_in=w_in_i, b_gate=bg, qg=qg, kg=kg,
                                   gsum=gsum)
        post = functools.partial(
            _post_call,
            w_dw=jnp.broadcast_to(w_dw[i][:, None, :], (CONV_K, SUBLANES, CONV_WIDTH)),
            b_dw=b_dw[i].reshape(1, -1),
            ln_g=conv_ln_g[i].reshape(1, -1), ln_b=conv_ln_b[i].reshape(1, -1),
            w_co=w_conv_out[i].astype(BF16), w_ao=w_attn_out[i].astype(BF16),
            w_o=w_out[i].astype(BF16), n2=norm2_g[i].reshape(1, D_MODEL),
            w1=w_mlp1[i].astype(BF16), w2=w_mlp2[i].astype(BF16))
        attn_any = functools.partial(_attn_call, lam_q=lam_q[i], lam_k=lam_k[i],
                                     norm_g=jnp.broadcast_to(attn_norm_g[i][:, None], (V_DIM, LANES)),
                                     lam_init=lam_init)
        score_bound = QK_DIM * jnp.max(jnp.abs(qg)) * jnp.max(jnp.abs(kg)) * SCORE_BOUND_MARGIN

        def attn(qt, sources, *, tq, heads, attn_any=attn_any, score_bound=score_bound):
            return lax.cond(score_bound <= SAFE_LOG2_SCORE,
                            lambda: attn_any(qt, sources, tq=tq, heads=heads, bounded=True),
                            lambda: attn_any(qt, sources, tq=tq, heads=heads, bounded=False))

        if last:
            k_c, vt_c = inproj(h_ctx, sc1, ac1, cos=cos_c, sin=sin_c, tm=tm_c, use_rope=False,
                               kv_only=True)
        else:
            z_c, qt_c, k_c, vt_c, gate_c = inproj(h_ctx, sc1, ac1, cos=cos_c, sin=sin_c,
                                                  tm=tm_c, use_rope=False)
        z_x, qt_x, k_x, vt_x, gate_x = inproj(x, sx1, ax1, cos=cos, sin=sin,
                                              tm=tm_x, use_rope=True)
        attn_x = attn(qt_x, [(k_x, vt_x), (k_c, vt_c)], tq=tq_x, heads=1)
        x = post(x, z_x, attn_x, gate_x, gx1, sx2, ax2, gx2, tm=tm_x)
        if not last:
            attn_c = attn(qt_c, [(k_c, vt_c)], tq=tq_c, heads=N_HEADS)
            h_ctx = post(h_ctx, z_c, attn_c, gate_c, gc1, sc2, ac2, gc2, tm=tm_c)
    return x
```

```python
import functools
import math

import jax
import jax.numpy as jnp
from jax import lax
from jax.experimental import pallas as pl
from jax.experimental.pallas import tpu as pltpu

D_MODEL = 1024
GRID_W = 64
CONV_WIDTH = 512
CONV_K = 31
N_HEADS = 8
QK_DIM = 64
V_DIM = 2 * QK_DIM
ATTN_WIDTH = N_HEADS * V_DIM
QK_COLS = N_HEADS * 2 * QK_DIM
D_FF = 4 * D_MODEL
ROPE_BASE = 10000.0
N_FREQ_AXIS = QK_DIM // 4
EPS = 1e-6
LOG2_E = math.log2(math.e)
SAFE_LOG2_SCORE = 60.0
SCORE_BOUND_MARGIN = 1.02

CONV_OFF = 0
Q_OFF = CONV_OFF + 2 * CONV_WIDTH
K_OFF = Q_OFF + QK_COLS
V_OFF = K_OFF + QK_COLS
GATE_OFF = V_OFF + ATTN_WIDTH
IN_COLS = GATE_OFF + 2 * D_MODEL

LANES = 128
SUBLANES = 8
BF16_ROWS = 16
HALO = 16
CONV_PAD = CONV_K // 2
ADA_ROWS = 24
VMEM_LIMIT = 56 * 1024 * 1024

BF16 = jnp.bfloat16
F32 = jnp.float32


def _const_spec(shape):
    zeros = (0,) * len(shape)
    return pl.BlockSpec(shape, lambda *_: zeros, pipeline_mode=pl.Buffered(1))


def _params(n_axes):
    return pltpu.CompilerParams(dimension_semantics=("parallel",) * n_axes,
                                vmem_limit_bytes=VMEM_LIMIT)


def _dot(a, b):
    return jnp.dot(a, b, preferred_element_type=F32)


def _sigmoid(t):
    return 1.0 / (1.0 + jnp.exp(-t))


def _norm_modulate(x, g, shift, scale):
    ms = jnp.mean(x * x, axis=-1, keepdims=True)
    h = x * lax.rsqrt(ms + EPS) * g
    return h * (1.0 + scale) + shift


def _ada_kernel(c_ref, w_ref, b_ref, o_ref):
    cv = c_ref[...]
    s = cv * _sigmoid(cv)
    o_ref[0] = _dot(s, w_ref[0]) + b_ref[0]


def _ada_call(cv, w_ada, b_ada):
    depth = w_ada.shape[0]
    n_col = 6
    return pl.pallas_call(
        _ada_kernel,
        grid=(depth, n_col),
        in_specs=[
            pl.BlockSpec((ADA_ROWS, D_MODEL), lambda i, j: (0, 0)),
            pl.BlockSpec((1, D_MODEL, D_MODEL), lambda i, j: (i, 0, j)),
            pl.BlockSpec((1, 1, D_MODEL), lambda i, j: (i, 0, j)),
        ],
        out_specs=pl.BlockSpec((1, ADA_ROWS, D_MODEL), lambda i, j: (i, 0, j)),
        out_shape=jax.ShapeDtypeStruct((depth, ADA_ROWS, n_col * D_MODEL), F32),
        compiler_params=_params(2),
        name="ada",
    )(cv, w_ada, b_ada.reshape(depth, 1, n_col * D_MODEL))


def _inproj_kernel(x_ref, shift_ref, scale_ref, g_ref, w_ref, bg_ref, qgt_ref, kg_ref,
                   gsum_ref, cos_ref, sin_ref, cost_ref, sint_ref, *out_refs, use_rope, kv_only):
    if kv_only:
        k_ref, vt_ref = out_refs
    else:
        z_ref, qt_ref, k_ref, vt_ref, gate_ref = out_refs
    h = _norm_modulate(x_ref[0], g_ref[...], shift_ref[0], scale_ref[0]).astype(BF16)
    tm = h.shape[0]

    lane = lax.broadcasted_iota(jnp.int32, (tm, LANES), 1)
    first_half = (lane & (QK_DIM // 2)) == 0

    def qk_head(t, gain):
        ms = _dot((t * t).astype(BF16), gsum_ref[...])
        t = t * lax.rsqrt(ms + EPS) * gain
        if use_rope:
            partner = jnp.where(first_half,
                                pltpu.roll(t, LANES - QK_DIM // 2, 1),
                                pltpu.roll(t, QK_DIM // 2, 1))
            t = t * cos_ref[...] + partner * sin_ref[...]
        return t

    if not kv_only:
        u = _dot(h, w_ref[:, CONV_OFF:Q_OFF])
        z_ref[0] = (u[:, :CONV_WIDTH] * _sigmoid(u[:, CONV_WIDTH:])).astype(BF16)

        tq = _dot(h, w_ref[:, Q_OFF:K_OFF])
        gain_t = jnp.tile(qgt_ref[...], (1, tm // LANES))
        half = QK_DIM // 2
        for hd in range(N_HEADS):
            t = tq[:, hd * LANES:(hd + 1) * LANES].T
            parts = []
            for c0 in range(0, LANES, QK_DIM):
                tc = t[c0:c0 + QK_DIM]
                ms = jnp.mean(tc * tc, axis=0, keepdims=True)
                tc = tc * lax.rsqrt(ms + EPS) * gain_t[c0:c0 + QK_DIM]
                if use_rope:
                    lo, hi = tc[:half], tc[half:]
                    parts += [lo * cost_ref[...] - hi * sint_ref[...],
                              hi * cost_ref[...] + lo * sint_ref[...]]
                else:
                    parts.append(tc)
            qt_ref[0, hd] = jnp.concatenate(parts, axis=0).astype(BF16)

    tk = _dot(h, w_ref[:, K_OFF:V_OFF])
    for hd in range(N_HEADS):
        t = qk_head(tk[:, hd * LANES:(hd + 1) * LANES], kg_ref[...])
        k_ref[0, :, hd * LANES:(hd + 1) * LANES] = t.astype(BF16)

    tv = _dot(h, w_ref[:, V_OFF:GATE_OFF])
    for hd in range(N_HEADS):
        vt_ref[0, hd, 0] = tv[:, hd * LANES:(hd + 1) * LANES].T.astype(BF16)

    if not kv_only:
        tg = _dot(h, w_ref[:, GATE_OFF:IN_COLS]) + bg_ref[...]
        gate_ref[0] = _sigmoid(tg).astype(BF16)


def _inproj_call(xs, shift, scale, g, w_in, b_gate, qgt, kg, gsum, cos, sin, cos_t, sin_t, *,
                 tm, use_rope, kv_only=False):
    B, L, _ = xs.shape
    nt = L // tm
    row = lambda b, i: (b, i, 0)
    per_batch = pl.BlockSpec((1, 1, D_MODEL), lambda b, i: (b, 0, 0))
    rope_spec = pl.BlockSpec((tm, LANES), lambda b, i: (i, 0))
    rope_t_spec = pl.BlockSpec((QK_DIM // 2, tm), lambda b, i: (0, i))
    k_spec = pl.BlockSpec((1, tm, QK_COLS), row)
    k_shape = jax.ShapeDtypeStruct((B, L, QK_COLS), BF16)
    vt_spec = pl.BlockSpec((1, N_HEADS, 1, V_DIM, tm), lambda b, i: (b, 0, i, 0, 0))
    vt_shape = jax.ShapeDtypeStruct((B, N_HEADS, nt, V_DIM, tm), BF16)
    if kv_only:
        out_specs, out_shape = [k_spec, vt_spec], [k_shape, vt_shape]
    else:
        out_specs = [
            pl.BlockSpec((1, tm, CONV_WIDTH), row),
            pl.BlockSpec((1, N_HEADS, V_DIM, tm), lambda b, i: (b, 0, 0, i)),
            k_spec, vt_spec,
            pl.BlockSpec((1, tm, 2 * D_MODEL), row),
        ]
        out_shape = [
            jax.ShapeDtypeStruct((B, L, CONV_WIDTH), BF16),
            jax.ShapeDtypeStruct((B, N_HEADS, V_DIM, L), BF16),
            k_shape, vt_shape,
            jax.ShapeDtypeStruct((B, L, 2 * D_MODEL), BF16),
        ]
    return pl.pallas_call(
        functools.partial(_inproj_kernel, use_rope=use_rope, kv_only=kv_only),
        grid=(B, nt),
        in_specs=[
            pl.BlockSpec((1, tm, D_MODEL), row),
            per_batch, per_batch,
            _const_spec((1, D_MODEL)),
            _const_spec((D_MODEL, IN_COLS)),
            _const_spec((1, 2 * D_MODEL)),
            _const_spec((LANES, LANES)), _const_spec((1, LANES)),
            _const_spec((LANES, LANES)),
            rope_spec, rope_spec, rope_t_spec, rope_t_spec,
        ],
        out_specs=out_specs,
        out_shape=out_shape,
        compiler_params=_params(2),
        name="inproj",
    )(xs, shift, scale, g, w_in, b_gate, qgt, kg, gsum, cos, sin, cos_t, sin_t)


def _attn_kernel(*refs, n_src, lam_init, bounded):
    qt_ref = refs[0]
    src_refs = refs[1:1 + 2 * n_src]
    lq_ref, lk_ref, ng_ref, o_ref = refs[1 + 2 * n_src:]
    heads, tq = qt_ref.shape[1], qt_ref.shape[3]

    lq = lq_ref[...]
    lk = lk_ref[...]
    dots = jnp.sum(lq * lk, axis=-1, keepdims=True)
    e = jnp.exp(dots)
    lam = e[0:1] - e[1:2] + lam_init
    gain = jnp.tile(ng_ref[...], (1, tq // LANES))

    chunks = []
    for si in range(n_src):
        k_ref, vt_ref = src_refs[2 * si], src_refs[2 * si + 1]
        n_chunks, tk = vt_ref.shape[2], vt_ref.shape[4]
        chunks += [(k_ref, vt_ref, ci, tk) for ci in range(n_chunks)]

    for hd in range(heads):
        qt = qt_ref[0, hd]
        sub = lax.broadcasted_iota(jnp.int32, qt.shape, 0)
        zero = jnp.zeros_like(qt)
        qm = jnp.concatenate([jnp.where(sub < QK_DIM, qt, zero),
                              jnp.where(sub >= QK_DIM, qt, zero)], axis=1)

        def scores(chunk, qm=qm, hd=hd):
            k_ref, _, ci, tk = chunk
            return _dot(k_ref[0, ci * tk:(ci + 1) * tk, hd * LANES:(hd + 1) * LANES], qm)

        m = None
        acc = None
        s_next = scores(chunks[0])
        for idx, (_, vt_ref, ci, tk) in enumerate(chunks):
            s = s_next
            if idx + 1 < len(chunks):
                s_next = scores(chunks[idx + 1])
            ones_rows = (lax.broadcasted_iota(jnp.int32, (BF16_ROWS, tk), 0) == 0).astype(BF16)
            vtc = jnp.concatenate([vt_ref[0, hd, ci], ones_rows], axis=0)
            if bounded:
                pv = _dot(vtc, jnp.exp2(s).astype(BF16))
                acc = pv if acc is None else acc + pv
            else:
                s_max = jnp.max(s, axis=0, keepdims=True)
                m_new = s_max if m is None else jnp.maximum(m, s_max)
                pv = _dot(vtc, jnp.exp2(s - m_new).astype(BF16))
                acc = pv if acc is None else jnp.exp2(m - m_new) * acc + pv
                m = m_new

        inv = 1.0 / acc[V_DIM:V_DIM + 1, :]
        o = acc[:V_DIM, :tq] * inv[:, :tq] - lam * (acc[:V_DIM, tq:] * inv[:, tq:])
        ms = jnp.mean(o * o, axis=0, keepdims=True)
        o = o * lax.rsqrt(ms + EPS) * gain * (1.0 - lam_init)
        o_ref[0, hd] = o.astype(BF16)


def _attn_call(qt, sources, lam_q, lam_k, norm_g, *, tq, heads, lam_init, bounded):
    B, _, _, L = qt.shape
    in_specs = [pl.BlockSpec((1, heads, V_DIM, tq), lambda b, h, i: (b, h, 0, i))]
    args = [qt]
    for k, vt in sources:
        T = k.shape[1]
        n_chunks, tk = vt.shape[2], vt.shape[4]
        in_specs.append(pl.BlockSpec((1, T, heads * LANES), lambda b, h, i: (b, 0, h)))
        in_specs.append(pl.BlockSpec((1, heads, n_chunks, V_DIM, tk),
                                     lambda b, h, i: (b, h, 0, 0, 0)))
        args += [k, vt]
    in_specs += [_const_spec((2, QK_DIM)), _const_spec((2, QK_DIM)), _const_spec((V_DIM, LANES))]
    args += [lam_q, lam_k, norm_g]
    return pl.pallas_call(
        functools.partial(_attn_kernel, n_src=len(sources), lam_init=lam_init, bounded=bounded),
        grid=(B, N_HEADS // heads, L // tq),
        in_specs=in_specs,
        out_specs=pl.BlockSpec((1, heads, V_DIM, tq), lambda b, h, i: (b, h, 0, i)),
        out_shape=jax.ShapeDtypeStruct((B, N_HEADS, V_DIM, L), BF16),
        compiler_params=_params(3),
        name="attn",
    )(*args)


CONV_ROWS = 32
FF_CHUNK = 1024
MERGE_ROWS = 256


def _post_kernel(x_ref, z_ref, zprev_ref, znext_ref, attn_ref, gate_ref, g1_ref,
                 shift2_ref, scale2_ref, g2_ref,
                 wdw_ref, bdw_ref, lng_ref, lnb_ref, wco_ref, wao_ref, wo_ref,
                 n2_ref, w1_ref, w2_ref, o_ref, zext_ref):
    i = pl.program_id(1)
    n = pl.num_programs(1)
    tm = z_ref.shape[1]

    prev = zprev_ref[0].astype(F32)
    nxt = znext_ref[0].astype(F32)
    zext_ref[0, 0:HALO, :] = jnp.where(i > 0, prev, 0.0)
    zext_ref[0, HALO:HALO + tm, :] = z_ref[0].astype(F32)
    zext_ref[0, HALO + tm:, :] = jnp.where(i < n - 1, nxt, 0.0)
    n_shifted = tm + 2 * HALO - SUBLANES
    for b in range(1, SUBLANES):
        zext_ref[b, 0:n_shifted, :] = zext_ref[0, b:b + n_shifted, :]

    def conv_act(base):
        cv = jnp.broadcast_to(bdw_ref[...], (CONV_ROWS // SUBLANES, SUBLANES, CONV_WIDTH))
        for k in range(CONV_K):
            a, b = divmod(HALO - CONV_PAD + k, SUBLANES)
            start = base + a * SUBLANES
            zk = zext_ref[b, start:start + CONV_ROWS, :]
            cv = cv + zk.reshape(CONV_ROWS // SUBLANES, SUBLANES, CONV_WIDTH) * wdw_ref[k][None]
        cv = cv.reshape(CONV_ROWS, CONV_WIDTH)
        mu = jnp.mean(cv, axis=-1, keepdims=True)
        cc = cv - mu
        var = jnp.mean(cc * cc, axis=-1, keepdims=True)
        y = cc * lax.rsqrt(var + EPS) * lng_ref[...] + lnb_ref[...]
        return (y * _sigmoid(y)).astype(BF16)

    group = min(MERGE_ROWS, tm)
    for r0 in range(0, tm, group):
        rows = slice(r0, r0 + group)
        yact = jnp.concatenate([conv_act(r0 + c) for c in range(0, group, CONV_ROWS)], axis=0)
        y_conv = _dot(yact, wco_ref[...])
        attn_rows = jnp.concatenate([attn_ref[0, hd, :, rows].T for hd in range(N_HEADS)], axis=1)
        y_attn = _dot(attn_rows, wao_ref[...])
        gates = gate_ref[0, rows, :].astype(F32)
        merged = gates[:, :D_MODEL] * y_conv + gates[:, D_MODEL:] * y_attn
        y_out = _dot(merged.astype(BF16), wo_ref[...])
        x1 = x_ref[0, rows, :] + g1_ref[0] * y_out
        h = _norm_modulate(x1, n2_ref[...], shift2_ref[0], scale2_ref[0]).astype(BF16)
        y = jnp.zeros(x1.shape, F32)
        for c in range(D_FF // FF_CHUNK):
            u = jnp.maximum(_dot(h, w1_ref[:, c * FF_CHUNK:(c + 1) * FF_CHUNK]), 0.0)
            y = y + _dot((u * u).astype(BF16), w2_ref[c * FF_CHUNK:(c + 1) * FF_CHUNK, :])
        o_ref[0, rows, :] = x1 + g2_ref[0] * y


def _post_call(xs, z, attn, gates, g1, shift2, scale2, g2, w_dw, b_dw, ln_g, ln_b, w_co, w_ao,
               w_o, n2, w1, w2, *, tm):
    B, L, _ = xs.shape
    nt = L // tm
    hb = tm // HALO
    last_hb = L // HALO - 1
    row = lambda b, i: (b, i, 0)
    per_batch = pl.BlockSpec((1, 1, D_MODEL), lambda b, i: (b, 0, 0))
    return pl.pallas_call(
        _post_kernel,
        grid=(B, nt),
        in_specs=[
            pl.BlockSpec((1, tm, D_MODEL), row),
            pl.BlockSpec((1, tm, CONV_WIDTH), row),
            pl.BlockSpec((1, HALO, CONV_WIDTH), lambda b, i: (b, jnp.maximum(i * hb - 1, 0), 0)),
            pl.BlockSpec((1, HALO, CONV_WIDTH), lambda b, i: (b, jnp.minimum((i + 1) * hb, last_hb), 0)),
            pl.BlockSpec((1, N_HEADS, V_DIM, tm), lambda b, i: (b, 0, 0, i)),
            pl.BlockSpec((1, tm, 2 * D_MODEL), row),
            per_batch, per_batch, per_batch, per_batch,
            _const_spec((CONV_K, SUBLANES, CONV_WIDTH)),
            _const_spec((1, CONV_WIDTH)), _const_spec((1, CONV_WIDTH)), _const_spec((1, CONV_WIDTH)),
            _const_spec((CONV_WIDTH, D_MODEL)),
            _const_spec((ATTN_WIDTH, D_MODEL)),
            _const_spec((D_MODEL, D_MODEL)),
            _const_spec((1, D_MODEL)),
            _const_spec((D_MODEL, D_FF)),
            _const_spec((D_FF, D_MODEL)),
        ],
        out_specs=pl.BlockSpec((1, tm, D_MODEL), row),
        out_shape=jax.ShapeDtypeStruct((B, L, D_MODEL), F32),
        scratch_shapes=[pltpu.VMEM((SUBLANES, tm + 2 * HALO, CONV_WIDTH), F32)],
        compiler_params=_params(2),
        name="post",
    )(xs, z, z, z, attn, gates, g1, shift2, scale2, g2, w_dw, b_dw, ln_g, ln_b, w_co, w_ao, w_o,
      n2, w1, w2)


def _rope_tables(n_tokens):
    rows = n_tokens // GRID_W
    row = jnp.repeat(jnp.arange(rows, dtype=F32), GRID_W)
    col = jnp.tile(jnp.arange(GRID_W, dtype=F32), rows)
    inv = jnp.power(ROPE_BASE, -jnp.arange(N_FREQ_AXIS, dtype=F32) / N_FREQ_AXIS)
    ang = jnp.concatenate([row[:, None] * inv, col[:, None] * inv], axis=-1)
    cos, sin = jnp.cos(ang), jnp.sin(ang)
    cos_k = jnp.tile(cos, (1, 4))
    sin_k = jnp.tile(jnp.concatenate([-sin, sin], axis=-1), (1, 2))
    return cos_k, sin_k, cos.T, sin.T


def _tiles(seq_len, ctx_len):
    latent = dict(tm=min(512, seq_len), tq=min(2048, seq_len))
    context = dict(tm=ctx_len, tq=ctx_len)
    return latent, context


def kernel(x, c, ctx, c_ctx, w_ada, b_ada, norm1_g, w_in, b_gate, q_norm_g, k_norm_g,
           lam_q, lam_k, attn_norm_g, w_dw, b_dw, conv_ln_g, conv_ln_b, w_conv_out,
           w_attn_out, w_out, norm2_g, w_mlp1, w_mlp2):
    B, S, _ = x.shape
    C = ctx.shape[1]
    depth = w_ada.shape[0]
    tile_x, tile_c = _tiles(S, C)
    tm_x, tq_x = tile_x["tm"], tile_x["tq"]
    tm_c, tq_c = tile_c["tm"], tile_c["tq"]

    cv = jnp.concatenate([c, c_ctx[None, :], jnp.zeros((ADA_ROWS - B - 1, D_MODEL), F32)], axis=0)
    mod = _ada_call(cv, w_ada, b_ada)

    cos, sin, cos_t, sin_t = _rope_tables(S)
    cos_c = jnp.ones((C, LANES), F32)
    sin_c = jnp.zeros((C, LANES), F32)
    cos_tc = jnp.ones((QK_DIM // 2, C), F32)
    sin_tc = jnp.zeros((QK_DIM // 2, C), F32)
    grp = jnp.arange(LANES) // QK_DIM
    gsum = jnp.where(grp[:, None] == grp[None, :], 1.0 / QK_DIM, 0.0).astype(BF16)
    qk_scale = QK_DIM ** -0.5

    h_ctx = ctx
    for i in range(depth):
        last = i == depth - 1
        lam_init = 0.8 - 0.6 * math.exp(-0.3 * i)
        mx = [t.reshape(B, 1, D_MODEL) for t in jnp.split(mod[i, :B], 6, axis=-1)]
        mc = [jnp.broadcast_to(t.reshape(1, 1, D_MODEL), (B, 1, D_MODEL))
              for t in jnp.split(mod[i, B], 6, axis=-1)]
        sx1, ax1, gx1, sx2, ax2, gx2 = mx
        sc1, ac1, gc1, sc2, ac2, gc2 = mc

        w_in_i = w_in[i].astype(BF16)
        bg = b_gate[i].reshape(1, 2 * D_MODEL)
        g1 = norm1_g[i].reshape(1, D_MODEL)
        qg = jnp.tile(q_norm_g[i], 2).reshape(1, LANES) * (qk_scale * LOG2_E)
        kg = jnp.tile(k_norm_g[i], 2).reshape(1, LANES)
        qgt = jnp.broadcast_to(qg.reshape(LANES, 1), (LANES, LANES))
        inproj = functools.partial(_inproj_call, g=g1, w_in=w_in_i, b_gate=bg, qgt=qgt, kg=kg,
                                   gsum=gsum)
        post = functools.partial(
            _post_call,
            w_dw=jnp.broadcast_to(w_dw[i][:, None, :], (CONV_K, SUBLANES, CONV_WIDTH)),
            b_dw=b_dw[i].reshape(1, -1),
            ln_g=conv_ln_g[i].reshape(1, -1), ln_b=conv_ln_b[i].reshape(1, -1),
            w_co=w_conv_out[i].astype(BF16), w_ao=w_attn_out[i].astype(BF16),
            w_o=w_out[i].astype(BF16), n2=norm2_g[i].reshape(1, D_MODEL),
            w1=w_mlp1[i].astype(BF16), w2=w_mlp2[i].astype(BF16))
        attn_any = functools.partial(_attn_call, lam_q=lam_q[i], lam_k=lam_k[i],
                                     norm_g=jnp.broadcast_to(attn_norm_g[i][:, None], (V_DIM, LANES)),
                                     lam_init=lam_init)
        score_bound = QK_DIM * jnp.max(jnp.abs(qg)) * jnp.max(jnp.abs(kg)) * SCORE_BOUND_MARGIN

        def attn(qt, sources, *, tq, heads, attn_any=attn_any, score_bound=score_bound):
            return lax.cond(score_bound <= SAFE_LOG2_SCORE,
                            lambda: attn_any(qt, sources, tq=tq, heads=heads, bounded=True),
                            lambda: attn_any(qt, sources, tq=tq, heads=heads, bounded=False))

        if last:
            k_c, vt_c = inproj(h_ctx, sc1, ac1, cos=cos_c, sin=sin_c, cos_t=cos_tc, sin_t=sin_tc,
                               tm=tm_c, use_rope=False, kv_only=True)
        else:
            z_c, qt_c, k_c, vt_c, gate_c = inproj(h_ctx, sc1, ac1, cos=cos_c, sin=sin_c,
                                                  cos_t=cos_tc, sin_t=sin_tc,
                                                  tm=tm_c, use_rope=False)
        z_x, qt_x, k_x, vt_x, gate_x = inproj(x, sx1, ax1, cos=cos, sin=sin, cos_t=cos_t,
                                              sin_t=sin_t, tm=tm_x, use_rope=True)
        attn_x = attn(qt_x, [(k_x, vt_x), (k_c, vt_c)], tq=tq_x, heads=1)
        x = post(x, z_x, attn_x, gate_x, gx1, sx2, ax2, gx2, tm=tm_x)
        if not last:
            attn_c = attn(qt_c, [(k_c, vt_c)], tq=tq_c, heads=N_HEADS)
            h_ctx = post(h_ctx, z_c, attn_c, gate_c, gc1, sc2, ac2, gc2, tm=tm_c)
    return x
```

```python
import functools
import math

import jax
import jax.numpy as jnp
from jax import lax
from jax.experimental import pallas as pl
from jax.experimental.pallas import tpu as pltpu

D_MODEL = 1024
GRID_W = 64
CONV_WIDTH = 512
CONV_K = 31
N_HEADS = 8
QK_DIM = 64
V_DIM = 2 * QK_DIM
ATTN_WIDTH = N_HEADS * V_DIM
QK_COLS = N_HEADS * 2 * QK_DIM
D_FF = 4 * D_MODEL
ROPE_BASE = 10000.0
N_FREQ_AXIS = QK_DIM // 4
EPS = 1e-6
LOG2_E = math.log2(math.e)
SAFE_LOG2_SCORE = 60.0
SCORE_BOUND_MARGIN = 1.02

CONV_OFF = 0
Q_OFF = CONV_OFF + 2 * CONV_WIDTH
K_OFF = Q_OFF + QK_COLS
V_OFF = K_OFF + QK_COLS
GATE_OFF = V_OFF + ATTN_WIDTH
IN_COLS = GATE_OFF + 2 * D_MODEL

LANES = 128
SUBLANES = 8
BF16_ROWS = 16
HALO = 16
CONV_PAD = CONV_K // 2
ADA_ROWS = 24
VMEM_LIMIT = 56 * 1024 * 1024

BF16 = jnp.bfloat16
F32 = jnp.float32


def _const_spec(shape):
    zeros = (0,) * len(shape)
    return pl.BlockSpec(shape, lambda *_: zeros, pipeline_mode=pl.Buffered(1))


def _params(n_axes):
    return pltpu.CompilerParams(dimension_semantics=("parallel",) * n_axes,
                                vmem_limit_bytes=VMEM_LIMIT)


def _dot(a, b):
    return jnp.dot(a, b, preferred_element_type=F32)


def _sigmoid(t):
    return 1.0 / (1.0 + jnp.exp(-t))


def _norm_modulate(x, g, shift, scale):
    ms = jnp.mean(x * x, axis=-1, keepdims=True)
    h = x * lax.rsqrt(ms + EPS) * g
    return h * (1.0 + scale) + shift


def _ada_kernel(c_ref, w_ref, b_ref, o_ref):
    cv = c_ref[...]
    s = cv * _sigmoid(cv)
    o_ref[0] = _dot(s, w_ref[0]) + b_ref[0]


def _ada_call(cv, w_ada, b_ada):
    depth = w_ada.shape[0]
    n_col = 6
    return pl.pallas_call(
        _ada_kernel,
        grid=(depth, n_col),
        in_specs=[
            pl.BlockSpec((ADA_ROWS, D_MODEL), lambda i, j: (0, 0)),
            pl.BlockSpec((1, D_MODEL, D_MODEL), lambda i, j: (i, 0, j)),
            pl.BlockSpec((1, 1, D_MODEL), lambda i, j: (i, 0, j)),
        ],
        out_specs=pl.BlockSpec((1, ADA_ROWS, D_MODEL), lambda i, j: (i, 0, j)),
        out_shape=jax.ShapeDtypeStruct((depth, ADA_ROWS, n_col * D_MODEL), F32),
        compiler_params=_params(2),
        name="ada",
    )(cv, w_ada, b_ada.reshape(depth, 1, n_col * D_MODEL))


def _inproj_kernel(x_ref, shift_ref, scale_ref, g_ref, w_ref, bg_ref, qgt_ref, kg_ref,
                   cos_ref, sin_ref, cost_ref, sint_ref, *out_refs, use_rope, kv_only):
    if kv_only:
        k_ref, vt_ref = out_refs
    else:
        z_ref, qt_ref, k_ref, vt_ref, gate_ref = out_refs
    h = _norm_modulate(x_ref[0], g_ref[...], shift_ref[0], scale_ref[0]).astype(BF16)
    tm = h.shape[0]

    lane = lax.broadcasted_iota(jnp.int32, (tm, LANES), 1)
    first_half = (lane & (QK_DIM // 2)) == 0
    first_comp = lane < QK_DIM

    def qk_head(t, gain):
        sq = t * t
        s_first = jnp.sum(jnp.where(first_comp, sq, 0.0), axis=-1, keepdims=True)
        s_all = jnp.sum(sq, axis=-1, keepdims=True)
        ms = jnp.where(first_comp, s_first, s_all - s_first) * (1.0 / QK_DIM)
        t = t * lax.rsqrt(ms + EPS) * gain
        if use_rope:
            partner = jnp.where(first_half,
                                pltpu.roll(t, LANES - QK_DIM // 2, 1),
                                pltpu.roll(t, QK_DIM // 2, 1))
            t = t * cos_ref[...] + partner * sin_ref[...]
        return t

    if not kv_only:
        u = _dot(h, w_ref[:, CONV_OFF:Q_OFF])
        z_ref[0] = (u[:, :CONV_WIDTH] * _sigmoid(u[:, CONV_WIDTH:])).astype(BF16)

        tq = _dot(h, w_ref[:, Q_OFF:K_OFF])
        gain_t = jnp.tile(qgt_ref[...], (1, tm // LANES))
        half = QK_DIM // 2
        for hd in range(N_HEADS):
            t = tq[:, hd * LANES:(hd + 1) * LANES].T
            parts = []
            for c0 in range(0, LANES, QK_DIM):
                tc = t[c0:c0 + QK_DIM]
                ms = jnp.mean(tc * tc, axis=0, keepdims=True)
                tc = tc * lax.rsqrt(ms + EPS) * gain_t[c0:c0 + QK_DIM]
                if use_rope:
                    lo, hi = tc[:half], tc[half:]
                    parts += [lo * cost_ref[...] - hi * sint_ref[...],
                              hi * cost_ref[...] + lo * sint_ref[...]]
                else:
                    parts.append(tc)
            qt_ref[0, hd] = jnp.concatenate(parts, axis=0).astype(BF16)

    tk = _dot(h, w_ref[:, K_OFF:V_OFF])
    for hd in range(N_HEADS):
        t = qk_head(tk[:, hd * LANES:(hd + 1) * LANES], kg_ref[...])
        k_ref[0, :, hd * LANES:(hd + 1) * LANES] = t.astype(BF16)

    tv = _dot(h, w_ref[:, V_OFF:GATE_OFF])
    for hd in range(N_HEADS):
        vt_ref[0, hd, 0] = tv[:, hd * LANES:(hd + 1) * LANES].T.astype(BF16)

    if not kv_only:
        tg = _dot(h, w_ref[:, GATE_OFF:IN_COLS]) + bg_ref[...]
        gate_ref[0] = _sigmoid(tg).astype(BF16)


def _inproj_call(xs, shift, scale, g, w_in, b_gate, qgt, kg, cos, sin, cos_t, sin_t, *,
                 tm, use_rope, kv_only=False):
    B, L, _ = xs.shape
    nt = L // tm
    row = lambda b, i: (b, i, 0)
    per_batch = pl.BlockSpec((1, 1, D_MODEL), lambda b, i: (b, 0, 0))
    rope_spec = pl.BlockSpec((tm, LANES), lambda b, i: (i, 0))
    rope_t_spec = pl.BlockSpec((QK_DIM // 2, tm), lambda b, i: (0, i))
    k_spec = pl.BlockSpec((1, tm, QK_COLS), row)
    k_shape = jax.ShapeDtypeStruct((B, L, QK_COLS), BF16)
    vt_spec = pl.BlockSpec((1, N_HEADS, 1, V_DIM, tm), lambda b, i: (b, 0, i, 0, 0))
    vt_shape = jax.ShapeDtypeStruct((B, N_HEADS, nt, V_DIM, tm), BF16)
    if kv_only:
        out_specs, out_shape = [k_spec, vt_spec], [k_shape, vt_shape]
    else:
        out_specs = [
            pl.BlockSpec((1, tm, CONV_WIDTH), row),
            pl.BlockSpec((1, N_HEADS, V_DIM, tm), lambda b, i: (b, 0, 0, i)),
            k_spec, vt_spec,
            pl.BlockSpec((1, tm, 2 * D_MODEL), row),
        ]
        out_shape = [
            jax.ShapeDtypeStruct((B, L, CONV_WIDTH), BF16),
            jax.ShapeDtypeStruct((B, N_HEADS, V_DIM, L), BF16),
            k_shape, vt_shape,
            jax.ShapeDtypeStruct((B, L, 2 * D_MODEL), BF16),
        ]
    return pl.pallas_call(
        functools.partial(_inproj_kernel, use_rope=use_rope, kv_only=kv_only),
        grid=(B, nt),
        in_specs=[
            pl.BlockSpec((1, tm, D_MODEL), row),
            per_batch, per_batch,
            _const_spec((1, D_MODEL)),
            _const_spec((D_MODEL, IN_COLS)),
            _const_spec((1, 2 * D_MODEL)),
            _const_spec((LANES, LANES)), _const_spec((1, LANES)),
            rope_spec, rope_spec, rope_t_spec, rope_t_spec,
        ],
        out_specs=out_specs,
        out_shape=out_shape,
        compiler_params=_params(2),
        name="inproj",
    )(xs, shift, scale, g, w_in, b_gate, qgt, kg, cos, sin, cos_t, sin_t)


def _attn_kernel(*refs, n_src, lam_init, bounded):
    qt_ref = refs[0]
    src_refs = refs[1:1 + 2 * n_src]
    lq_ref, lk_ref, ng_ref, o_ref = refs[1 + 2 * n_src:]
    heads, tq = qt_ref.shape[1], qt_ref.shape[3]

    lq = lq_ref[...]
    lk = lk_ref[...]
    dots = jnp.sum(lq * lk, axis=-1, keepdims=True)
    e = jnp.exp(dots)
    lam = e[0:1] - e[1:2] + lam_init
    gain = jnp.tile(ng_ref[...], (1, tq // LANES))

    chunks = []
    for si in range(n_src):
        k_ref, vt_ref = src_refs[2 * si], src_refs[2 * si + 1]
        n_chunks, tk = vt_ref.shape[2], vt_ref.shape[4]
        chunks += [(k_ref, vt_ref, ci, tk) for ci in range(n_chunks)]

    for hd in range(heads):
        qt = qt_ref[0, hd]
        sub = lax.broadcasted_iota(jnp.int32, qt.shape, 0)
        zero = jnp.zeros_like(qt)
        qm = jnp.concatenate([jnp.where(sub < QK_DIM, qt, zero),
                              jnp.where(sub >= QK_DIM, qt, zero)], axis=1)

        def scores(chunk, qm=qm, hd=hd):
            k_ref, _, ci, tk = chunk
            return _dot(k_ref[0, ci * tk:(ci + 1) * tk, hd * LANES:(hd + 1) * LANES], qm)

        m = None
        acc = None
        s_next = scores(chunks[0])
        for idx, (_, vt_ref, ci, tk) in enumerate(chunks):
            s = s_next
            if idx + 1 < len(chunks):
                s_next = scores(chunks[idx + 1])
            ones_rows = (lax.broadcasted_iota(jnp.int32, (BF16_ROWS, tk), 0) == 0).astype(BF16)
            vtc = jnp.concatenate([vt_ref[0, hd, ci], ones_rows], axis=0)
            if bounded:
                pv = _dot(vtc, jnp.exp2(s).astype(BF16))
                acc = pv if acc is None else acc + pv
            else:
                s_max = jnp.max(s, axis=0, keepdims=True)
                m_new = s_max if m is None else jnp.maximum(m, s_max)
                pv = _dot(vtc, jnp.exp2(s - m_new).astype(BF16))
                acc = pv if acc is None else jnp.exp2(m - m_new) * acc + pv
                m = m_new

        inv = 1.0 / acc[V_DIM:V_DIM + 1, :]
        o = acc[:V_DIM, :tq] * inv[:, :tq] - lam * (acc[:V_DIM, tq:] * inv[:, tq:])
        ms = jnp.mean(o * o, axis=0, keepdims=True)
        o = o * lax.rsqrt(ms + EPS) * gain * (1.0 - lam_init)
        o_ref[0, hd] = o.astype(BF16)


def _attn_call(qt, sources, lam_q, lam_k, norm_g, *, tq, heads, lam_init, bounded):
    B, _, _, L = qt.shape
    in_specs = [pl.BlockSpec((1, heads, V_DIM, tq), lambda b, h, i: (b, h, 0, i))]
    args = [qt]
    for k, vt in sources:
        T = k.shape[1]
        n_chunks, tk = vt.shape[2], vt.shape[4]
        in_specs.append(pl.BlockSpec((1, T, heads * LANES), lambda b, h, i: (b, 0, h)))
        in_specs.append(pl.BlockSpec((1, heads, n_chunks, V_DIM, tk),
                                     lambda b, h, i: (b, h, 0, 0, 0)))
        args += [k, vt]
    in_specs += [_const_spec((2, QK_DIM)), _const_spec((2, QK_DIM)), _const_spec((V_DIM, LANES))]
    args += [lam_q, lam_k, norm_g]
    return pl.pallas_call(
        functools.partial(_attn_kernel, n_src=len(sources), lam_init=lam_init, bounded=bounded),
        grid=(B, N_HEADS // heads, L // tq),
        in_specs=in_specs,
        out_specs=pl.BlockSpec((1, heads, V_DIM, tq), lambda b, h, i: (b, h, 0, i)),
        out_shape=jax.ShapeDtypeStruct((B, N_HEADS, V_DIM, L), BF16),
        compiler_params=_params(3),
        name="attn",
    )(*args)


CONV_ROWS = 32
FF_CHUNK = 1024
MERGE_ROWS = 256


def _post_kernel(x_ref, z_ref, zprev_ref, znext_ref, attn_ref, gate_ref, g1_ref,
                 shift2_ref, scale2_ref, g2_ref,
                 wdw_ref, bdw_ref, lng_ref, lnb_ref, wco_ref, wao_ref, wo_ref,
                 n2_ref, w1_ref, w2_ref, o_ref, zext_ref):
    i = pl.program_id(1)
    n = pl.num_programs(1)
    tm = z_ref.shape[1]

    prev = zprev_ref[0].astype(F32)
    nxt = znext_ref[0].astype(F32)
    zext_ref[0, 0:HALO, :] = jnp.where(i > 0, prev, 0.0)
    zext_ref[0, HALO:HALO + tm, :] = z_ref[0].astype(F32)
    zext_ref[0, HALO + tm:, :] = jnp.where(i < n - 1, nxt, 0.0)
    n_shifted = tm + 2 * HALO - SUBLANES
    for b in range(1, SUBLANES):
        zext_ref[b, 0:n_shifted, :] = zext_ref[0, b:b + n_shifted, :]

    def conv_act(base):
        cv = jnp.broadcast_to(bdw_ref[...], (CONV_ROWS // SUBLANES, SUBLANES, CONV_WIDTH))
        for k in range(CONV_K):
            a, b = divmod(HALO - CONV_PAD + k, SUBLANES)
            start = base + a * SUBLANES
            zk = zext_ref[b, start:start + CONV_ROWS, :]
            cv = cv + zk.reshape(CONV_ROWS // SUBLANES, SUBLANES, CONV_WIDTH) * wdw_ref[k][None]
        cv = cv.reshape(CONV_ROWS, CONV_WIDTH)
        mu = jnp.mean(cv, axis=-1, keepdims=True)
        cc = cv - mu
        var = jnp.mean(cc * cc, axis=-1, keepdims=True)
        y = cc * lax.rsqrt(var + EPS) * lng_ref[...] + lnb_ref[...]
        return (y * _sigmoid(y)).astype(BF16)

    group = min(MERGE_ROWS, tm)
    for r0 in range(0, tm, group):
        rows = slice(r0, r0 + group)
        yact = jnp.concatenate([conv_act(r0 + c) for c in range(0, group, CONV_ROWS)], axis=0)
        y_conv = _dot(yact, wco_ref[...])
        attn_rows = jnp.concatenate([attn_ref[0, hd, :, rows].T for hd in range(N_HEADS)], axis=1)
        y_attn = _dot(attn_rows, wao_ref[...])
        gates = gate_ref[0, rows, :].astype(F32)
        merged = gates[:, :D_MODEL] * y_conv + gates[:, D_MODEL:] * y_attn
        y_out = _dot(merged.astype(BF16), wo_ref[...])
        x1 = x_ref[0, rows, :] + g1_ref[0] * y_out
        h = _norm_modulate(x1, n2_ref[...], shift2_ref[0], scale2_ref[0]).astype(BF16)
        y = jnp.zeros(x1.shape, F32)
        for c in range(D_FF // FF_CHUNK):
            u = jnp.maximum(_dot(h, w1_ref[:, c * FF_CHUNK:(c + 1) * FF_CHUNK]), 0.0)
            y = y + _dot((u * u).astype(BF16), w2_ref[c * FF_CHUNK:(c + 1) * FF_CHUNK, :])
        o_ref[0, rows, :] = x1 + g2_ref[0] * y


def _post_call(xs, z, attn, gates, g1, shift2, scale2, g2, w_dw, b_dw, ln_g, ln_b, w_co, w_ao,
               w_o, n2, w1, w2, *, tm):
    B, L, _ = xs.shape
    nt = L // tm
    hb = tm // HALO
    last_hb = L // HALO - 1
    row = lambda b, i: (b, i, 0)
    per_batch = pl.BlockSpec((1, 1, D_MODEL), lambda b, i: (b, 0, 0))
    return pl.pallas_call(
        _post_kernel,
        grid=(B, nt),
        in_specs=[
            pl.BlockSpec((1, tm, D_MODEL), row),
            pl.BlockSpec((1, tm, CONV_WIDTH), row),
            pl.BlockSpec((1, HALO, CONV_WIDTH), lambda b, i: (b, jnp.maximum(i * hb - 1, 0), 0)),
            pl.BlockSpec((1, HALO, CONV_WIDTH), lambda b, i: (b, jnp.minimum((i + 1) * hb, last_hb), 0)),
            pl.BlockSpec((1, N_HEADS, V_DIM, tm), lambda b, i: (b, 0, 0, i)),
            pl.BlockSpec((1, tm, 2 * D_MODEL), row),
            per_batch, per_batch, per_batch, per_batch,
            _const_spec((CONV_K, SUBLANES, CONV_WIDTH)),
            _const_spec((1, CONV_WIDTH)), _const_spec((1, CONV_WIDTH)), _const_spec((1, CONV_WIDTH)),
            _const_spec((CONV_WIDTH, D_MODEL)),
            _const_spec((ATTN_WIDTH, D_MODEL)),
            _const_spec((D_MODEL, D_MODEL)),
            _const_spec((1, D_MODEL)),
            _const_spec((D_MODEL, D_FF)),
            _const_spec((D_FF, D_MODEL)),
        ],
        out_specs=pl.BlockSpec((1, tm, D_MODEL), row),
        out_shape=jax.ShapeDtypeStruct((B, L, D_MODEL), F32),
        scratch_shapes=[pltpu.VMEM((SUBLANES, tm + 2 * HALO, CONV_WIDTH), F32)],
        compiler_params=_params(2),
        name="post",
    )(xs, z, z, z, attn, gates, g1, shift2, scale2, g2, w_dw, b_dw, ln_g, ln_b, w_co, w_ao, w_o,
      n2, w1, w2)


def _rope_tables(n_tokens):
    rows = n_tokens // GRID_W
    row = jnp.repeat(jnp.arange(rows, dtype=F32), GRID_W)
    col = jnp.tile(jnp.arange(GRID_W, dtype=F32), rows)
    inv = jnp.power(ROPE_BASE, -jnp.arange(N_FREQ_AXIS, dtype=F32) / N_FREQ_AXIS)
    ang = jnp.concatenate([row[:, None] * inv, col[:, None] * inv], axis=-1)
    cos, sin = jnp.cos(ang), jnp.sin(ang)
    cos_k = jnp.tile(cos, (1, 4))
    sin_k = jnp.tile(jnp.concatenate([-sin, sin], axis=-1), (1, 2))
    return cos_k, sin_k, cos.T, sin.T


def _tiles(seq_len, ctx_len):
    latent = dict(tm=min(512, seq_len), tq=min(2048, seq_len))
    context = dict(tm=ctx_len, tq=ctx_len)
    return latent, context


def kernel(x, c, ctx, c_ctx, w_ada, b_ada, norm1_g, w_in, b_gate, q_norm_g, k_norm_g,
           lam_q, lam_k, attn_norm_g, w_dw, b_dw, conv_ln_g, conv_ln_b, w_conv_out,
           w_attn_out, w_out, norm2_g, w_mlp1, w_mlp2):
    B, S, _ = x.shape
    C = ctx.shape[1]
    depth = w_ada.shape[0]
    tile_x, tile_c = _tiles(S, C)
    tm_x, tq_x = tile_x["tm"], tile_x["tq"]
    tm_c, tq_c = tile_c["tm"], tile_c["tq"]

    cv = jnp.concatenate([c, c_ctx[None, :], jnp.zeros((ADA_ROWS - B - 1, D_MODEL), F32)], axis=0)
    mod = _ada_call(cv, w_ada, b_ada)

    cos, sin, cos_t, sin_t = _rope_tables(S)
    cos_c = jnp.ones((C, LANES), F32)
    sin_c = jnp.zeros((C, LANES), F32)
    cos_tc = jnp.ones((QK_DIM // 2, C), F32)
    sin_tc = jnp.zeros((QK_DIM // 2, C), F32)
    qk_scale = QK_DIM ** -0.5

    h_ctx = ctx
    for i in range(depth):
        last = i == depth - 1
        lam_init = 0.8 - 0.6 * math.exp(-0.3 * i)
        mx = [t.reshape(B, 1, D_MODEL) for t in jnp.split(mod[i, :B], 6, axis=-1)]
        mc = [jnp.broadcast_to(t.reshape(1, 1, D_MODEL), (B, 1, D_MODEL))
              for t in jnp.split(mod[i, B], 6, axis=-1)]
        sx1, ax1, gx1, sx2, ax2, gx2 = mx
        sc1, ac1, gc1, sc2, ac2, gc2 = mc

        w_in_i = w_in[i].astype(BF16)
        bg = b_gate[i].reshape(1, 2 * D_MODEL)
        g1 = norm1_g[i].reshape(1, D_MODEL)
        qg = jnp.tile(q_norm_g[i], 2).reshape(1, LANES) * (qk_scale * LOG2_E)
        kg = jnp.tile(k_norm_g[i], 2).reshape(1, LANES)
        qgt = jnp.broadcast_to(qg.reshape(LANES, 1), (LANES, LANES))
        inproj = functools.partial(_inproj_call, g=g1, w_in=w_in_i, b_gate=bg, qgt=qgt, kg=kg)
        post = functools.partial(
            _post_call,
            w_dw=jnp.broadcast_to(w_dw[i][:, None, :], (CONV_K, SUBLANES, CONV_WIDTH)),
            b_dw=b_dw[i].reshape(1, -1),
            ln_g=conv_ln_g[i].reshape(1, -1), ln_b=conv_ln_b[i].reshape(1, -1),
            w_co=w_conv_out[i].astype(BF16), w_ao=w_attn_out[i].astype(BF16),
            w_o=w_out[i].astype(BF16), n2=norm2_g[i].reshape(1, D_MODEL),
            w1=w_mlp1[i].astype(BF16), w2=w_mlp2[i].astype(BF16))
        attn_any = functools.partial(_attn_call, lam_q=lam_q[i], lam_k=lam_k[i],
                                     norm_g=jnp.broadcast_to(attn_norm_g[i][:, None], (V_DIM, LANES)),
                                     lam_init=lam_init)
        score_bound = QK_DIM * jnp.max(jnp.abs(qg)) * jnp.max(jnp.abs(kg)) * SCORE_BOUND_MARGIN

        def attn(qt, sources, *, tq, heads, attn_any=attn_any, score_bound=score_bound):
            return lax.cond(score_bound <= SAFE_LOG2_SCORE,
                            lambda: attn_any(qt, sources, tq=tq, heads=heads, bounded=True),
                            lambda: attn_any(qt, sources, tq=tq, heads=heads, bounded=False))

        if last:
            k_c, vt_c = inproj(h_ctx, sc1, ac1, cos=cos_c, sin=sin_c, cos_t=cos_tc, sin_t=sin_tc,
                               tm=tm_c, use_rope=False, kv_only=True)
        else:
            z_c, qt_c, k_c, vt_c, gate_c = inproj(h_ctx, sc1, ac1, cos=cos_c, sin=sin_c,
                                                  cos_t=cos_tc, sin_t=sin_tc,
                                                  tm=tm_c, use_rope=False)
        z_x, qt_x, k_x, vt_x, gate_x = inproj(x, sx1, ax1, cos=cos, sin=sin, cos_t=cos_t,
                                              sin_t=sin_t, tm=tm_x, use_rope=True)
        attn_x = attn(qt_x, [(k_x, vt_x), (k_c, vt_c)], tq=tq_x, heads=1)
        x = post(x, z_x, attn_x, gate_x, gx1, sx2, ax2, gx2, tm=tm_x)
        if not last:
            attn_c = attn(qt_c, [(k_c, vt_c)], tq=tq_c, heads=N_HEADS)
            h_ctx = post(h_ctx, z_c, attn_c, gate_c, gc1, sc2, ac2, gc2, tm=tm_c)
    return x
```

```python
import functools
import math

import jax
import jax.numpy as jnp
from jax import lax
from jax.experimental import pallas as pl
from jax.experimental.pallas import tpu as pltpu

D_MODEL = 1024
GRID_W = 64
CONV_WIDTH = 512
CONV_K = 31
N_HEADS = 8
QK_DIM = 64
V_DIM = 2 * QK_DIM
ATTN_WIDTH = N_HEADS * V_DIM
QK_COLS = N_HEADS * 2 * QK_DIM
D_FF = 4 * D_MODEL
ROPE_BASE = 10000.0
N_FREQ_AXIS = QK_DIM // 4
EPS = 1e-6
LOG2_E = math.log2(math.e)
SAFE_LOG2_SCORE = 60.0
SCORE_BOUND_MARGIN = 1.02

CONV_OFF = 0
Q_OFF = CONV_OFF + 2 * CONV_WIDTH
K_OFF = Q_OFF + QK_COLS
V_OFF = K_OFF + QK_COLS
GATE_OFF = V_OFF + ATTN_WIDTH
IN_COLS = GATE_OFF + 2 * D_MODEL

LANES = 128
SUBLANES = 8
BF16_ROWS = 16
HALO = 16
CONV_PAD = CONV_K // 2
ADA_ROWS = 24
VMEM_LIMIT = 56 * 1024 * 1024

BF16 = jnp.bfloat16
F32 = jnp.float32


def _const_spec(shape):
    zeros = (0,) * len(shape)
    return pl.BlockSpec(shape, lambda *_: zeros, pipeline_mode=pl.Buffered(1))


def _params(n_axes):
    return pltpu.CompilerParams(dimension_semantics=("parallel",) * n_axes,
                                vmem_limit_bytes=VMEM_LIMIT)


def _dot(a, b):
    return jnp.dot(a, b, preferred_element_type=F32)


def _sigmoid(t):
    return 1.0 / (1.0 + jnp.exp(-t))


def _norm_modulate(x, g, shift, scale):
    ms = jnp.mean(x * x, axis=-1, keepdims=True)
    h = x * lax.rsqrt(ms + EPS) * g
    return h * (1.0 + scale) + shift


def _ada_kernel(c_ref, w_ref, b_ref, o_ref):
    cv = c_ref[...]
    s = cv * _sigmoid(cv)
    o_ref[0] = _dot(s, w_ref[0]) + b_ref[0]


def _ada_call(cv, w_ada, b_ada):
    depth = w_ada.shape[0]
    n_col = 6
    return pl.pallas_call(
        _ada_kernel,
        grid=(depth, n_col),
        in_specs=[
            pl.BlockSpec((ADA_ROWS, D_MODEL), lambda i, j: (0, 0)),
            pl.BlockSpec((1, D_MODEL, D_MODEL), lambda i, j: (i, 0, j)),
            pl.BlockSpec((1, 1, D_MODEL), lambda i, j: (i, 0, j)),
        ],
        out_specs=pl.BlockSpec((1, ADA_ROWS, D_MODEL), lambda i, j: (i, 0, j)),
        out_shape=jax.ShapeDtypeStruct((depth, ADA_ROWS, n_col * D_MODEL), F32),
        compiler_params=_params(2),
        name="ada",
    )(cv, w_ada, b_ada.reshape(depth, 1, n_col * D_MODEL))


def _inproj_kernel(x_ref, shift_ref, scale_ref, g_ref, w_ref, bg_ref, qgt_ref, kg_ref,
                   cos_ref, sin_ref, cost_ref, sint_ref, *out_refs, use_rope, kv_only):
    if kv_only:
        k_ref, vt_ref = out_refs
    else:
        z_ref, qt_ref, k_ref, vt_ref, gate_ref = out_refs
    h = _norm_modulate(x_ref[0], g_ref[...], shift_ref[0], scale_ref[0]).astype(BF16)
    tm = h.shape[0]

    lane = lax.broadcasted_iota(jnp.int32, (tm, LANES), 1)
    first_half = (lane & (QK_DIM // 2)) == 0
    first_comp = lane < QK_DIM

    def qk_head(t, gain):
        sq = t * t
        s_first = jnp.sum(jnp.where(first_comp, sq, 0.0), axis=-1, keepdims=True)
        s_all = jnp.sum(sq, axis=-1, keepdims=True)
        ms = jnp.where(first_comp, s_first, s_all - s_first) * (1.0 / QK_DIM)
        t = t * lax.rsqrt(ms + EPS) * gain
        if use_rope:
            partner = jnp.where(first_half,
                                pltpu.roll(t, LANES - QK_DIM // 2, 1),
                                pltpu.roll(t, QK_DIM // 2, 1))
            t = t * cos_ref[...] + partner * sin_ref[...]
        return t

    if not kv_only:
        u = _dot(h, w_ref[:, CONV_OFF:Q_OFF])
        z_ref[0] = (u[:, :CONV_WIDTH] * _sigmoid(u[:, CONV_WIDTH:])).astype(BF16)

        tq = _dot(h, w_ref[:, Q_OFF:K_OFF])
        gain_t = jnp.tile(qgt_ref[...], (1, tm // LANES))
        half = QK_DIM // 2
        for hd in range(N_HEADS):
            t = tq[:, hd * LANES:(hd + 1) * LANES].T
            parts = []
            for c0 in range(0, LANES, QK_DIM):
                tc = t[c0:c0 + QK_DIM]
                ms = jnp.mean(tc * tc, axis=0, keepdims=True)
                tc = tc * lax.rsqrt(ms + EPS) * gain_t[c0:c0 + QK_DIM]
                if use_rope:
                    lo, hi = tc[:half], tc[half:]
                    parts += [lo * cost_ref[...] - hi * sint_ref[...],
                              hi * cost_ref[...] + lo * sint_ref[...]]
                else:
                    parts.append(tc)
            qt_ref[0, hd] = jnp.concatenate(parts, axis=0).astype(BF16)

    tk = _dot(h, w_ref[:, K_OFF:V_OFF])
    for hd in range(N_HEADS):
        t = qk_head(tk[:, hd * LANES:(hd + 1) * LANES], kg_ref[...])
        k_ref[0, :, hd * LANES:(hd + 1) * LANES] = t.astype(BF16)

    tv = _dot(h, w_ref[:, V_OFF:GATE_OFF])
    for hd in range(N_HEADS):
        vt_ref[0, hd, 0] = tv[:, hd * LANES:(hd + 1) * LANES].T.astype(BF16)

    if not kv_only:
        tg = _dot(h, w_ref[:, GATE_OFF:IN_COLS]) + bg_ref[...]
        gate_ref[0] = _sigmoid(tg).astype(BF16)


def _inproj_call(xs, shift, scale, g, w_in, b_gate, qgt, kg, cos, sin, cos_t, sin_t, *,
                 tm, use_rope, kv_only=False):
    B, L, _ = xs.shape
    nt = L // tm
    row = lambda b, i: (b, i, 0)
    per_batch = pl.BlockSpec((1, 1, D_MODEL), lambda b, i: (b, 0, 0))
    rope_spec = pl.BlockSpec((tm, LANES), lambda b, i: (i, 0))
    rope_t_spec = pl.BlockSpec((QK_DIM // 2, tm), lambda b, i: (0, i))
    k_spec = pl.BlockSpec((1, tm, QK_COLS), row)
    k_shape = jax.ShapeDtypeStruct((B, L, QK_COLS), BF16)
    vt_spec = pl.BlockSpec((1, N_HEADS, 1, V_DIM, tm), lambda b, i: (b, 0, i, 0, 0))
    vt_shape = jax.ShapeDtypeStruct((B, N_HEADS, nt, V_DIM, tm), BF16)
    if kv_only:
        out_specs, out_shape = [k_spec, vt_spec], [k_shape, vt_shape]
    else:
        out_specs = [
            pl.BlockSpec((1, tm, CONV_WIDTH), row),
            pl.BlockSpec((1, N_HEADS, V_DIM, tm), lambda b, i: (b, 0, 0, i)),
            k_spec, vt_spec,
            pl.BlockSpec((1, tm, 2 * D_MODEL), row),
        ]
        out_shape = [
            jax.ShapeDtypeStruct((B, L, CONV_WIDTH), BF16),
            jax.ShapeDtypeStruct((B, N_HEADS, V_DIM, L), BF16),
            k_shape, vt_shape,
            jax.ShapeDtypeStruct((B, L, 2 * D_MODEL), BF16),
        ]
    return pl.pallas_call(
        functools.partial(_inproj_kernel, use_rope=use_rope, kv_only=kv_only),
        grid=(B, nt),
        in_specs=[
            pl.BlockSpec((1, tm, D_MODEL), row),
            per_batch, per_batch,
            _const_spec((1, D_MODEL)),
            _const_spec((D_MODEL, IN_COLS)),
            _const_spec((1, 2 * D_MODEL)),
            _const_spec((LANES, LANES)), _const_spec((1, LANES)),
            rope_spec, rope_spec, rope_t_spec, rope_t_spec,
        ],
        out_specs=out_specs,
        out_shape=out_shape,
        compiler_params=_params(2),
        name="inproj",
    )(xs, shift, scale, g, w_in, b_gate, qgt, kg, cos, sin, cos_t, sin_t)


def _attn_kernel(*refs, n_src, lam_init, bounded):
    qt_ref = refs[0]
    src_refs = refs[1:1 + 2 * n_src]
    lq_ref, lk_ref, ng_ref, o_ref = refs[1 + 2 * n_src:]
    heads, tq = qt_ref.shape[1], qt_ref.shape[3]

    lq = lq_ref[...]
    lk = lk_ref[...]
    dots = jnp.sum(lq * lk, axis=-1, keepdims=True)
    e = jnp.exp(dots)
    lam = e[0:1] - e[1:2] + lam_init
    gain = jnp.tile(ng_ref[...], (1, tq // LANES))

    chunks = []
    for si in range(n_src):
        k_ref, vt_ref = src_refs[2 * si], src_refs[2 * si + 1]
        n_chunks, tk = vt_ref.shape[2], vt_ref.shape[4]
        chunks += [(k_ref, vt_ref, ci, tk) for ci in range(n_chunks)]

    for hd in range(heads):
        qt = qt_ref[0, hd]
        sub = lax.broadcasted_iota(jnp.int32, qt.shape, 0)
        zero = jnp.zeros_like(qt)
        qm = jnp.concatenate([jnp.where(sub < QK_DIM, qt, zero),
                              jnp.where(sub >= QK_DIM, qt, zero)], axis=1)

        def scores(chunk, qm=qm, hd=hd):
            k_ref, _, ci, tk = chunk
            return _dot(k_ref[0, ci * tk:(ci + 1) * tk, hd * LANES:(hd + 1) * LANES], qm)

        m = None
        acc = None
        s_next = scores(chunks[0])
        for idx, (_, vt_ref, ci, tk) in enumerate(chunks):
            s = s_next
            if idx + 1 < len(chunks):
                s_next = scores(chunks[idx + 1])
            ones_rows = (lax.broadcasted_iota(jnp.int32, (BF16_ROWS, tk), 0) == 0).astype(BF16)
            vtc = jnp.concatenate([vt_ref[0, hd, ci], ones_rows], axis=0)
            if bounded:
                pv = _dot(vtc, jnp.exp2(s).astype(BF16))
                acc = pv if acc is None else acc + pv
            else:
                s_max = jnp.max(s, axis=0, keepdims=True)
                m_new = s_max if m is None else jnp.maximum(m, s_max)
                pv = _dot(vtc, jnp.exp2(s - m_new).astype(BF16))
                acc = pv if acc is None else jnp.exp2(m - m_new) * acc + pv
                m = m_new

        inv = 1.0 / acc[V_DIM:V_DIM + 1, :]
        o = acc[:V_DIM, :tq] * inv[:, :tq] - lam * (acc[:V_DIM, tq:] * inv[:, tq:])
        ms = jnp.mean(o * o, axis=0, keepdims=True)
        o = o * lax.rsqrt(ms + EPS) * gain * (1.0 - lam_init)
        o_ref[0, hd] = o.astype(BF16)


def _attn_call(qt, sources, lam_q, lam_k, norm_g, *, tq, heads, lam_init, bounded):
    B, _, _, L = qt.shape
    in_specs = [pl.BlockSpec((1, heads, V_DIM, tq), lambda b, h, i: (b, h, 0, i))]
    args = [qt]
    for k, vt in sources:
        T = k.shape[1]
        n_chunks, tk = vt.shape[2], vt.shape[4]
        in_specs.append(pl.BlockSpec((1, T, heads * LANES), lambda b, h, i: (b, 0, h)))
        in_specs.append(pl.BlockSpec((1, heads, n_chunks, V_DIM, tk),
                                     lambda b, h, i: (b, h, 0, 0, 0)))
        args += [k, vt]
    in_specs += [_const_spec((2, QK_DIM)), _const_spec((2, QK_DIM)), _const_spec((V_DIM, LANES))]
    args += [lam_q, lam_k, norm_g]
    return pl.pallas_call(
        functools.partial(_attn_kernel, n_src=len(sources), lam_init=lam_init, bounded=bounded),
        grid=(B, N_HEADS // heads, L // tq),
        in_specs=in_specs,
        out_specs=pl.BlockSpec((1, heads, V_DIM, tq), lambda b, h, i: (b, h, 0, i)),
        out_shape=jax.ShapeDtypeStruct((B, N_HEADS, V_DIM, L), BF16),
        compiler_params=_params(3),
        name="attn",
    )(*args)


CONV_ROWS = 32
FF_CHUNK = 1024
MERGE_ROWS = 256
MERGE_COLS = 256


def _post_kernel(x_ref, z_ref, zprev_ref, znext_ref, attn_ref, gate_ref, g1_ref,
                 shift2_ref, scale2_ref, g2_ref,
                 wdw_ref, bdw_ref, lng_ref, lnb_ref, wco_ref, wao_ref, wo_ref,
                 n2_ref, w1_ref, w2_ref, o_ref, zext_ref):
    i = pl.program_id(1)
    n = pl.num_programs(1)
    tm = z_ref.shape[1]

    prev = zprev_ref[0].astype(F32)
    nxt = znext_ref[0].astype(F32)
    zext_ref[0, 0:HALO, :] = jnp.where(i > 0, prev, 0.0)
    zext_ref[0, HALO:HALO + tm, :] = z_ref[0].astype(F32)
    zext_ref[0, HALO + tm:, :] = jnp.where(i < n - 1, nxt, 0.0)
    n_shifted = tm + 2 * HALO - SUBLANES
    for b in range(1, SUBLANES):
        zext_ref[b, 0:n_shifted, :] = zext_ref[0, b:b + n_shifted, :]

    def conv_act(base):
        cv = jnp.broadcast_to(bdw_ref[...], (CONV_ROWS // SUBLANES, SUBLANES, CONV_WIDTH))
        for k in range(CONV_K):
            a, b = divmod(HALO - CONV_PAD + k, SUBLANES)
            start = base + a * SUBLANES
            zk = zext_ref[b, start:start + CONV_ROWS, :]
            cv = cv + zk.reshape(CONV_ROWS // SUBLANES, SUBLANES, CONV_WIDTH) * wdw_ref[k][None]
        cv = cv.reshape(CONV_ROWS, CONV_WIDTH)
        mu = jnp.mean(cv, axis=-1, keepdims=True)
        cc = cv - mu
        var = jnp.mean(cc * cc, axis=-1, keepdims=True)
        y = cc * lax.rsqrt(var + EPS) * lng_ref[...] + lnb_ref[...]
        return (y * _sigmoid(y)).astype(BF16)

    group = min(MERGE_ROWS, tm)
    for r0 in range(0, tm, group):
        rows = slice(r0, r0 + group)
        yact = jnp.concatenate([conv_act(r0 + c) for c in range(0, group, CONV_ROWS)], axis=0)
        attn_rows = jnp.concatenate([attn_ref[0, hd, :, rows].T for hd in range(N_HEADS)], axis=1)
        merged = []
        for c0 in range(0, D_MODEL, MERGE_COLS):
            cols = slice(c0, c0 + MERGE_COLS)
            g_conv = gate_ref[0, rows, cols].astype(F32)
            g_attn = gate_ref[0, rows, D_MODEL + c0:D_MODEL + c0 + MERGE_COLS].astype(F32)
            m = g_conv * _dot(yact, wco_ref[:, cols]) + g_attn * _dot(attn_rows, wao_ref[:, cols])
            merged.append(m.astype(BF16))
        y_out = _dot(jnp.concatenate(merged, axis=1), wo_ref[...])
        x1 = x_ref[0, rows, :] + g1_ref[0] * y_out
        h = _norm_modulate(x1, n2_ref[...], shift2_ref[0], scale2_ref[0]).astype(BF16)
        y = jnp.zeros(x1.shape, F32)
        for c in range(D_FF // FF_CHUNK):
            u = jnp.maximum(_dot(h, w1_ref[:, c * FF_CHUNK:(c + 1) * FF_CHUNK]), 0.0)
            y = y + _dot((u * u).astype(BF16), w2_ref[c * FF_CHUNK:(c + 1) * FF_CHUNK, :])
        o_ref[0, rows, :] = x1 + g2_ref[0] * y


def _post_call(xs, z, attn, gates, g1, shift2, scale2, g2, w_dw, b_dw, ln_g, ln_b, w_co, w_ao,
               w_o, n2, w1, w2, *, tm):
    B, L, _ = xs.shape
    nt = L // tm
    hb = tm // HALO
    last_hb = L // HALO - 1
    row = lambda b, i: (b, i, 0)
    per_batch = pl.BlockSpec((1, 1, D_MODEL), lambda b, i: (b, 0, 0))
    return pl.pallas_call(
        _post_kernel,
        grid=(B, nt),
        in_specs=[
            pl.BlockSpec((1, tm, D_MODEL), row),
            pl.BlockSpec((1, tm, CONV_WIDTH), row),
            pl.BlockSpec((1, HALO, CONV_WIDTH), lambda b, i: (b, jnp.maximum(i * hb - 1, 0), 0)),
            pl.BlockSpec((1, HALO, CONV_WIDTH), lambda b, i: (b, jnp.minimum((i + 1) * hb, last_hb), 0)),
            pl.BlockSpec((1, N_HEADS, V_DIM, tm), lambda b, i: (b, 0, 0, i)),
            pl.BlockSpec((1, tm, 2 * D_MODEL), row),
            per_batch, per_batch, per_batch, per_batch,
            _const_spec((CONV_K, SUBLANES, CONV_WIDTH)),
            _const_spec((1, CONV_WIDTH)), _const_spec((1, CONV_WIDTH)), _const_spec((1, CONV_WIDTH)),
            _const_spec((CONV_WIDTH, D_MODEL)),
            _const_spec((ATTN_WIDTH, D_MODEL)),
            _const_spec((D_MODEL, D_MODEL)),
            _const_spec((1, D_MODEL)),
            _const_spec((D_MODEL, D_FF)),
            _const_spec((D_FF, D_MODEL)),
        ],
        out_specs=pl.BlockSpec((1, tm, D_MODEL), row),
        out_shape=jax.ShapeDtypeStruct((B, L, D_MODEL), F32),
        scratch_shapes=[pltpu.VMEM((SUBLANES, tm + 2 * HALO, CONV_WIDTH), F32)],
        compiler_params=_params(2),
        name="post",
    )(xs, z, z, z, attn, gates, g1, shift2, scale2, g2, w_dw, b_dw, ln_g, ln_b, w_co, w_ao, w_o,
      n2, w1, w2)


def _rope_tables(n_tokens):
    rows = n_tokens // GRID_W
    row = jnp.repeat(jnp.arange(rows, dtype=F32), GRID_W)
    col = jnp.tile(jnp.arange(GRID_W, dtype=F32), rows)
    inv = jnp.power(ROPE_BASE, -jnp.arange(N_FREQ_AXIS, dtype=F32) / N_FREQ_AXIS)
    ang = jnp.concatenate([row[:, None] * inv, col[:, None] * inv], axis=-1)
    cos, sin = jnp.cos(ang), jnp.sin(ang)
    cos_k = jnp.tile(cos, (1, 4))
    sin_k = jnp.tile(jnp.concatenate([-sin, sin], axis=-1), (1, 2))
    return cos_k, sin_k, cos.T, sin.T


def _tiles(seq_len, ctx_len):
    latent = dict(tm=min(512, seq_len), tq=min(2048, seq_len))
    context = dict(tm=ctx_len, tq=ctx_len)
    return latent, context


def kernel(x, c, ctx, c_ctx, w_ada, b_ada, norm1_g, w_in, b_gate, q_norm_g, k_norm_g,
           lam_q, lam_k, attn_norm_g, w_dw, b_dw, conv_ln_g, conv_ln_b, w_conv_out,
           w_attn_out, w_out, norm2_g, w_mlp1, w_mlp2):
    B, S, _ = x.shape
    C = ctx.shape[1]
    depth = w_ada.shape[0]
    tile_x, tile_c = _tiles(S, C)
    tm_x, tq_x = tile_x["tm"], tile_x["tq"]
    tm_c, tq_c = tile_c["tm"], tile_c["tq"]

    cv = jnp.concatenate([c, c_ctx[None, :], jnp.zeros((ADA_ROWS - B - 1, D_MODEL), F32)], axis=0)
    mod = _ada_call(cv, w_ada, b_ada)

    cos, sin, cos_t, sin_t = _rope_tables(S)
    cos_c = jnp.ones((C, LANES), F32)
    sin_c = jnp.zeros((C, LANES), F32)
    cos_tc = jnp.ones((QK_DIM // 2, C), F32)
    sin_tc = jnp.zeros((QK_DIM // 2, C), F32)
    qk_scale = QK_DIM ** -0.5

    h_ctx = ctx
    for i in range(depth):
        last = i == depth - 1
        lam_init = 0.8 - 0.6 * math.exp(-0.3 * i)
        mx = [t.reshape(B, 1, D_MODEL) for t in jnp.split(mod[i, :B], 6, axis=-1)]
        mc = [jnp.broadcast_to(t.reshape(1, 1, D_MODEL), (B, 1, D_MODEL))
              for t in jnp.split(mod[i, B], 6, axis=-1)]
        sx1, ax1, gx1, sx2, ax2, gx2 = mx
        sc1, ac1, gc1, sc2, ac2, gc2 = mc

        w_in_i = w_in[i].astype(BF16)
        bg = b_gate[i].reshape(1, 2 * D_MODEL)
        g1 = norm1_g[i].reshape(1, D_MODEL)
        qg = jnp.tile(q_norm_g[i], 2).reshape(1, LANES) * (qk_scale * LOG2_E)
        kg = jnp.tile(k_norm_g[i], 2).reshape(1, LANES)
        qgt = jnp.broadcast_to(qg.reshape(LANES, 1), (LANES, LANES))
        inproj = functools.partial(_inproj_call, g=g1, w_in=w_in_i, b_gate=bg, qgt=qgt, kg=kg)
        post = functools.partial(
            _post_call,
            w_dw=jnp.broadcast_to(w_dw[i][:, None, :], (CONV_K, SUBLANES, CONV_WIDTH)),
            b_dw=b_dw[i].reshape(1, -1),
            ln_g=conv_ln_g[i].reshape(1, -1), ln_b=conv_ln_b[i].reshape(1, -1),
            w_co=w_conv_out[i].astype(BF16), w_ao=w_attn_out[i].astype(BF16),
            w_o=w_out[i].astype(BF16), n2=norm2_g[i].reshape(1, D_MODEL),
            w1=w_mlp1[i].astype(BF16), w2=w_mlp2[i].astype(BF16))
        attn_any = functools.partial(_attn_call, lam_q=lam_q[i], lam_k=lam_k[i],
                                     norm_g=jnp.broadcast_to(attn_norm_g[i][:, None], (V_DIM, LANES)),
                                     lam_init=lam_init)
        score_bound = QK_DIM * jnp.max(jnp.abs(qg)) * jnp.max(jnp.abs(kg)) * SCORE_BOUND_MARGIN

        def attn(qt, sources, *, tq, heads, attn_any=attn_any, score_bound=score_bound):
            return lax.cond(score_bound <= SAFE_LOG2_SCORE,
                            lambda: attn_any(qt, sources, tq=tq, heads=heads, bounded=True),
                            lambda: attn_any(qt, sources, tq=tq, heads=heads, bounded=False))

        if last:
            k_c, vt_c = inproj(h_ctx, sc1, ac1, cos=cos_c, sin=sin_c, cos_t=cos_tc, sin_t=sin_tc,
                               tm=tm_c, use_rope=False, kv_only=True)
        else:
            z_c, qt_c, k_c, vt_c, gate_c = inproj(h_ctx, sc1, ac1, cos=cos_c, sin=sin_c,
                                                  cos_t=cos_tc, sin_t=sin_tc,
                                                  tm=tm_c, use_rope=False)
        z_x, qt_x, k_x, vt_x, gate_x = inproj(x, sx1, ax1, cos=cos, sin=sin, cos_t=cos_t,
                                              sin_t=sin_t, tm=tm_x, use_rope=True)
        attn_x = attn(qt_x, [(k_x, vt_x), (k_c, vt_c)], tq=tq_x, heads=1)
        x = post(x, z_x, attn_x, gate_x, gx1, sx2, ax2, gx2, tm=tm_x)
        if not last:
            attn_c = attn(qt_c, [(k_c, vt_c)], tq=tq_c, heads=N_HEADS)
            h_ctx = post(h_ctx, z_c, attn_c, gate_c, gc1, sc2, ac2, gc2, tm=tm_c)
    return x
```
